```python
import math
import jax, jax.numpy as jnp
from jax import lax
import numpy as np

D_MODEL = 1024
BATCH = 16
SEQ = 2048
DEPTH = 1
DEC_BATCH = 32
DEC_SEQ = 8
PAST_LEN = 16384
PAGE_SIZE = 128

N_DIFF_HEADS = 8
DIFF_HEAD_DIM = 64
DIFF_V_DIM = 2 * DIFF_HEAD_DIM
N_SB_HEADS = 8
SB_HEAD_DIM = 64
ROPE_DIM = DIFF_HEAD_DIM // 4
ROPE_THETA = 500000.0
D_FF = 2816
CONV_WIDTH = 3
Q_BLOCK = 128
EPS = 1e-6
DIFF_QK_W = N_DIFF_HEADS * 2 * DIFF_HEAD_DIM
DIFF_V_W = N_DIFF_HEADS * DIFF_V_DIM
SB_W = N_SB_HEADS * SB_HEAD_DIM
IN_PROJ_W = 2 * DIFF_QK_W + DIFF_V_W + 3 * SB_W + 2 * D_MODEL

kernel_name = 'hybrid_diff_stickbreak_decoder_step'


def _in_split_points():
    widths = (DIFF_QK_W, DIFF_QK_W, DIFF_V_W, SB_W, SB_W, SB_W, D_MODEL)
    pts, acc = [], 0
    for w in widths:
        acc += w
        pts.append(acc)
    return pts


def _rmsnorm(x, g):
    xf = x.astype(jnp.float32)
    y = xf * lax.rsqrt(jnp.mean(xf * xf, axis=-1, keepdims=True) + EPS)
    return (y * g.astype(jnp.float32)).astype(x.dtype)


def _rope(x, pos):
    half = ROPE_DIM // 2
    inv_freq = jnp.power(jnp.float32(ROPE_THETA), -jnp.arange(half, dtype=jnp.float32) * (2.0 / ROPE_DIM))
    ang = pos.astype(jnp.float32)[:, None] * inv_freq[None, :]
    cos = jnp.cos(ang)[None, :, None, None, :]
    sin = jnp.sin(ang)[None, :, None, None, :]
    xf = x.astype(jnp.float32)
    x1 = xf[..., :half]
    x2 = xf[..., half:ROPE_DIM]
    out = jnp.concatenate([x1 * cos - x2 * sin, x1 * sin + x2 * cos, xf[..., ROPE_DIM:]], axis=-1)
    return out.astype(x.dtype)


def _diff_prompt(q, k, v, lam):
    S = q.shape[1]
    scale = DIFF_HEAD_DIM ** -0.5
    outs = []
    for start in range(0, S, Q_BLOCK):
        end = min(start + Q_BLOCK, S)
        s = jnp.einsum('bqhcd,bkhcd->bhcqk', q[:, start:end], k[:, :end]).astype(jnp.float32) * scale
        causal = jnp.arange(end)[None, :] <= jnp.arange(start, end)[:, None]
        p = jax.nn.softmax(jnp.where(causal, s, -jnp.inf), axis=-1)
        a = p[:, :, 0] - lam * p[:, :, 1]
        outs.append(jnp.einsum('bhqk,bkhd->bqhd', a.astype(v.dtype), v[:, :end]))
    return jnp.concatenate(outs, axis=1)


def _diff_sample(q, k, v, k_past, v_past, lam):
    T = q.shape[1]
    P = k_past.shape[1]
    scale = DIFF_HEAD_DIM ** -0.5
    s_past = jnp.einsum('bqhcd,bkhcd->bhcqk', q, k_past).astype(jnp.float32) * scale
    s_new = jnp.einsum('bqhcd,bkhcd->bhcqk', q, k).astype(jnp.float32) * scale
    causal = jnp.arange(T)[None, :] <= jnp.arange(T)[:, None]
    s_new = jnp.where(causal, s_new, -jnp.inf)
    p = jax.nn.softmax(jnp.concatenate([s_past, s_new], axis=-1), axis=-1)
    a = (p[:, :, 0] - lam * p[:, :, 1]).astype(v.dtype)
    return (jnp.einsum('bhqk,bkhd->bqhd', a[..., :P], v_past)
            + jnp.einsum('bhqk,bkhd->bqhd', a[..., P:], v))


def _sb_weights(z, valid):
    log_stay = jnp.where(valid, jax.nn.log_sigmoid(-z), 0.0)
    after = lax.cumsum(log_stay, axis=z.ndim - 1, reverse=True) - log_stay
    return jnp.where(valid, jnp.exp(jax.nn.log_sigmoid(z) + after), 0.0)


def _sb_prompt(q, k, v):
    S = q.shape[1]
    scale = SB_HEAD_DIM ** -0.5
    outs = []
    for start in range(0, S, Q_BLOCK):
        end = min(start + Q_BLOCK, S)
        z = jnp.einsum('bqhd,bkhd->bhqk', q[:, start:end], k[:, :end]).astype(jnp.float32) * scale
        valid = jnp.arange(end)[None, :] < jnp.arange(start, end)[:, None]
        w = _sb_weights(z, valid)
        outs.append(jnp.einsum('bhqk,bkhd->bqhd', w.astype(v.dtype), v[:, :end]))
    return jnp.concatenate(outs, axis=1)


def _sb_sample(q, k, v, k_past, v_past):
    T = q.shape[1]
    P = k_past.shape[1]
    scale = SB_HEAD_DIM ** -0.5
    z_past = jnp.einsum('bqhd,bkhd->bhqk', q, k_past).astype(jnp.float32) * scale
    z_new = jnp.einsum('bqhd,bkhd->bhqk', q, k).astype(jnp.float32) * scale
    z = jnp.concatenate([z_past, z_new], axis=-1)
    valid = jnp.concatenate([jnp.ones((T, P), dtype=bool),
                             jnp.arange(T)[None, :] < jnp.arange(T)[:, None]], axis=-1)
    w = _sb_weights(z, valid).astype(v.dtype)
    return (jnp.einsum('bhqk,bkhd->bqhd', w[..., :P], v_past)
            + jnp.einsum('bhqk,bkhd->bqhd', w[..., P:], v))


def _conv_ffn(h, conv_prev, w_up, conv_w, conv_b, w_down):
    T = h.shape[1]
    u = h @ w_up
    ext = jnp.concatenate([conv_prev.astype(u.dtype), u], axis=1)
    conv = conv_b
    for i in range(CONV_WIDTH):
        conv = conv + ext[:, i:i + T] * conv_w[i]
    gate, val = jnp.split(conv, 2, axis=-1)
    y = jax.nn.gelu(gate) * val
    return y @ w_down, ext[:, -(CONV_WIDTH - 1):]


def _layer(x, c, pos, past, conv_prev, layer_idx, w_ada, b_ada, g_pre_mix, g_post_mix, g_pre_ffn, g_post_ffn,
           w_in, lam_q1, lam_k1, lam_q2, lam_k2, g_subln, w_br_a, w_br_b, w_out, w_up, conv_w, conv_b, w_down):
    B, T, _ = x.shape
    ada = jax.nn.silu(c) @ w_ada + b_ada
    sh_m, sc_m, gt_m, sh_f, sc_f, gt_f = [a[:, None, :] for a in jnp.split(ada, 6, axis=-1)]
    h = _rmsnorm(x, g_pre_mix) * (1 + sc_m) + sh_m
    u = h @ w_in
    dq, dk, dv, sq, sk, sv, ga, gb = jnp.split(u, _in_split_points(), axis=-1)
    dq = _rope(dq.reshape(B, T, N_DIFF_HEADS, 2, DIFF_HEAD_DIM), pos)
    dk = _rope(dk.reshape(B, T, N_DIFF_HEADS, 2, DIFF_HEAD_DIM), pos)
    dv = dv.reshape(B, T, N_DIFF_HEADS, DIFF_V_DIM)
    sq = sq.reshape(B, T, N_SB_HEADS, SB_HEAD_DIM)
    sk = sk.reshape(B, T, N_SB_HEADS, SB_HEAD_DIM)
    sv = sv.reshape(B, T, N_SB_HEADS, SB_HEAD_DIM)
    lam_init = 0.8 - 0.6 * math.exp(-0.3 * layer_idx)
    lam = (jnp.exp(jnp.sum(lam_q1.astype(jnp.float32) * lam_k1.astype(jnp.float32)))
           - jnp.exp(jnp.sum(lam_q2.astype(jnp.float32) * lam_k2.astype(jnp.float32))) + lam_init)
    if past is None:
        od = _diff_prompt(dq, dk, dv, lam)
        osb = _sb_prompt(sq, sk, sv)
    else:
        kd_past, vd_past, ks_past, vs_past = past
        od = _diff_sample(dq, dk, dv, kd_past, vd_past, lam)
        osb = _sb_sample(sq, sk, sv, ks_past, vs_past)
    od = _rmsnorm(od, g_subln) * (1.0 - lam_init)
    ya = od.reshape(B, T, DIFF_V_W) @ w_br_a
    yb = osb.reshape(B, T, SB_W) @ w_br_b
    mix = (jax.nn.sigmoid(ga) * ya + jax.nn.sigmoid(gb) * yb) @ w_out
    x = x + gt_m * _rmsnorm(mix, g_post_mix)
    h2 = _rmsnorm(x, g_pre_ffn) * (1 + sc_f) + sh_f
    f, conv_state = _conv_ffn(h2, conv_prev, w_up, conv_w, conv_b, w_down)
    x = x + gt_f * _rmsnorm(f, g_post_ffn)
    return x, (dk.reshape(B, T, N_DIFF_HEADS, 2 * DIFF_HEAD_DIM), dv, sk, sv, conv_state)


def setup_inputs(seed: int = 0) -> dict:
    key = jax.random.key(seed)
    ks = jax.random.split(key, 40)
    f32 = jnp.float32
    n_pages = PAST_LEN // PAGE_SIZE
    n_used = DEC_BATCH * n_pages
    n_phys = n_used + n_used // 4
    nrm = lambda k, shape, s: jax.random.normal(k, shape, f32) * s
    page_table = jax.random.permutation(ks[9], n_phys)[:n_used].reshape(DEC_BATCH, n_pages).astype(jnp.int32)
    return {
        'x_prompt': nrm(ks[0], (BATCH, SEQ, D_MODEL), 1.0),
        'x_sample': nrm(ks[1], (DEC_BATCH, DEC_SEQ, D_MODEL), 1.0),
        'c_prompt': nrm(ks[2], (BATCH, D_MODEL), 1.0),
        'c_sample': nrm(ks[3], (DEC_BATCH, D_MODEL), 1.0),
        'cache_diff_k': nrm(ks[4], (DEPTH, n_phys, PAGE_SIZE, N_DIFF_HEADS, 2 * DIFF_HEAD_DIM), 1.0),
        'cache_diff_v': nrm(ks[5], (DEPTH, n_phys, PAGE_SIZE, N_DIFF_HEADS, DIFF_V_DIM), 1.0),
        'cache_sb_k': nrm(ks[6], (DEPTH, n_phys, PAGE_SIZE, N_SB_HEADS, SB_HEAD_DIM), 1.0),
        'cache_sb_v': nrm(ks[7], (DEPTH, n_phys, PAGE_SIZE, N_SB_HEADS, SB_HEAD_DIM), 1.0),
        'state_conv': nrm(ks[8], (DEPTH, DEC_BATCH, CONV_WIDTH - 1, 2 * D_FF), 0.5),
        'page_table': page_table,
        'w_ada': nrm(ks[10], (DEPTH, D_MODEL, 6 * D_MODEL), 0.5 * D_MODEL ** -0.5),
        'b_ada': nrm(ks[11], (DEPTH, 6 * D_MODEL), 0.02),
        'g_pre_mix': 1.0 + nrm(ks[12], (DEPTH, D_MODEL), 0.05),
        'g_post_mix': 1.0 + nrm(ks[13], (DEPTH, D_MODEL), 0.05),
        'g_pre_ffn': 1.0 + nrm(ks[14], (DEPTH, D_MODEL), 0.05),
        'g_post_ffn': 1.0 + nrm(ks[15], (DEPTH, D_MODEL), 0.05),
        'w_in': nrm(ks[16], (DEPTH, D_MODEL, IN_PROJ_W), D_MODEL ** -0.5),
        'lam_q1': nrm(ks[17], (DEPTH, DIFF_HEAD_DIM), 0.1),
        'lam_k1': nrm(ks[18], (DEPTH, DIFF_HEAD_DIM), 0.1),
        'lam_q2': nrm(ks[19], (DEPTH, DIFF_HEAD_DIM), 0.1),
        'lam_k2': nrm(ks[20], (DEPTH, DIFF_HEAD_DIM), 0.1),
        'g_subln': 1.0 + nrm(ks[21], (DEPTH, DIFF_V_DIM), 0.05),
        'w_br_a': nrm(ks[22], (DEPTH, DIFF_V_W, D_MODEL), DIFF_V_W ** -0.5),
        'w_br_b': nrm(ks[23], (DEPTH, SB_W, D_MODEL), SB_W ** -0.5),
        'w_out': nrm(ks[24], (DEPTH, D_MODEL, D_MODEL), D_MODEL ** -0.5),
        'w_up': nrm(ks[25], (DEPTH, D_MODEL, 2 * D_FF), D_MODEL ** -0.5),
        'conv_w': nrm(ks[26], (DEPTH, CONV_WIDTH, 2 * D_FF), CONV_WIDTH ** -0.5),
        'conv_b': nrm(ks[27], (DEPTH, 2 * D_FF), 0.02),
        'w_down': nrm(ks[28], (DEPTH, D_FF, D_MODEL), D_FF ** -0.5),
    }


def reference(x_prompt, x_sample, c_prompt, c_sample, cache_diff_k, cache_diff_v, cache_sb_k, cache_sb_v,
              state_conv, page_table, w_ada, b_ada, g_pre_mix, g_post_mix, g_pre_ffn, g_post_ffn, w_in,
              lam_q1, lam_k1, lam_q2, lam_k2, g_subln, w_br_a, w_br_b, w_out, w_up, conv_w, conv_b, w_down):
    B, S, _ = x_prompt.shape
    DB, T, _ = x_sample.shape
    n_past = page_table.shape[1] * cache_diff_k.shape[2]
    pos_prompt = jnp.arange(S, dtype=jnp.int32)
    pos_sample = n_past + jnp.arange(T, dtype=jnp.int32)
    hp, hs = x_prompt, x_sample
    new_p = [[], [], [], [], []]
    new_s = [[], [], [], [], []]
    for l in range(DEPTH):
        params = (w_ada[l], b_ada[l], g_pre_mix[l], g_post_mix[l], g_pre_ffn[l], g_post_ffn[l], w_in[l],
                  lam_q1[l], lam_k1[l], lam_q2[l], lam_k2[l], g_subln[l], w_br_a[l], w_br_b[l], w_out[l],
                  w_up[l], conv_w[l], conv_b[l], w_down[l])
        past = (cache_diff_k[l, page_table].reshape(DB, n_past, N_DIFF_HEADS, 2, DIFF_HEAD_DIM),
                cache_diff_v[l, page_table].reshape(DB, n_past, N_DIFF_HEADS, DIFF_V_DIM),
                cache_sb_k[l, page_table].reshape(DB, n_past, N_SB_HEADS, SB_HEAD_DIM),
                cache_sb_v[l, page_table].reshape(DB, n_past, N_SB_HEADS, SB_HEAD_DIM))
        zero_conv = jnp.zeros((B, CONV_WIDTH - 1, 2 * D_FF), hp.dtype)
        hp, st_p = _layer(hp, c_prompt, pos_prompt, None, zero_conv, l, *params)
        hs, st_s = _layer(hs, c_sample, pos_sample, past, state_conv[l], l, *params)
        for i in range(5):
            new_p[i].append(st_p[i])
            new_s[i].append(st_s[i])
    kd_p, vd_p, ks_p, vs_p, conv_p = [jnp.stack(a, axis=0) for a in new_p]
    kd_s, vd_s, ks_s, vs_s, conv_s = [jnp.stack(a, axis=0) for a in new_s]
    return (hp, hs, kd_p, vd_p, ks_p, vs_p, conv_p, kd_s, vd_s, ks_s, vs_s, conv_s)
```

```python
import functools
import math

import jax
import jax.numpy as jnp
from jax import lax
from jax.experimental import pallas as pl
from jax.experimental.pallas import tpu as pltpu

D_MODEL = 1024
N_DIFF_HEADS = 8
DIFF_HEAD_DIM = 64
DIFF_V_DIM = 2 * DIFF_HEAD_DIM
N_SB_HEADS = 8
SB_HEAD_DIM = 64
ROPE_DIM = DIFF_HEAD_DIM // 4
ROPE_HALF = ROPE_DIM // 2
ROPE_THETA = 500000.0
D_FF = 2816
CONV_WIDTH = 3
EPS = 1e-6
DIFF_QK_W = N_DIFF_HEADS * 2 * DIFF_HEAD_DIM
DIFF_V_W = N_DIFF_HEADS * DIFF_V_DIM
SB_W = N_SB_HEADS * SB_HEAD_DIM
IN_PROJ_W = 2 * DIFF_QK_W + DIFF_V_W + 3 * SB_W + 2 * D_MODEL
QK_SCALE = DIFF_HEAD_DIM ** -0.5

LANES = 128
SUBLANES = 8
VMEM_LIMIT_BYTES = 56 * 1024 * 1024

F32 = jnp.float32
BF16 = jnp.bfloat16


def _params(n_grid_dims, vmem=VMEM_LIMIT_BYTES):
    return pltpu.CompilerParams(dimension_semantics=("arbitrary",) * n_grid_dims, vmem_limit_bytes=vmem)


def _dot(a, b):
    return jnp.dot(a, b, preferred_element_type=F32)


def _dot_nt(a, b):
    return lax.dot_general(a, b, (((1,), (1,)), ((), ())), preferred_element_type=F32)


def _sigmoid(x):
    return 1.0 / (1.0 + jnp.exp(-x))


def _rms(x):
    return x * lax.rsqrt(jnp.mean(x * x, axis=-1, keepdims=True) + EPS)


def _log_sigmoid_neg(z):
    return -(jnp.maximum(z, 0.0) + jnp.log1p(jnp.exp(-jnp.abs(z))))


def _lam_from_params(lam_ref, lam_init):
    lp = lam_ref[...]
    s1 = jnp.sum(lp[0:1] * lp[1:2], axis=-1, keepdims=True)
    s2 = jnp.sum(lp[2:3] * lp[3:4], axis=-1, keepdims=True)
    return jnp.exp(s1) - jnp.exp(s2) + lam_init


def _suffix_matrix():
    j = lax.broadcasted_iota(jnp.int32, (LANES, LANES), 0)
    s = lax.broadcasted_iota(jnp.int32, (LANES, LANES), 1)
    return jnp.where(j > s, 1.0, 0.0).astype(BF16)


def _suffix_sum_exclusive(x, tri):
    hi = x.astype(BF16)
    lo = (x - hi.astype(F32)).astype(BF16)
    return _dot(hi, tri) + _dot(lo, tri)


def _ada_kernel(c_ref, w_ref, b_ref, o_ref):
    c = c_ref[...]
    a = (c * _sigmoid(c)).astype(BF16)
    o_ref[...] = _dot(a, w_ref[...]) + b_ref[...]


def _ada(c, w_bf16, b):
    n, d = c.shape
    n_out = w_bf16.shape[1]
    return pl.pallas_call(
        _ada_kernel,
        out_shape=jax.ShapeDtypeStruct((n, n_out), F32),
        grid=(n_out // d,),
        in_specs=[
            pl.BlockSpec((n, d), lambda j: (0, 0)),
            pl.BlockSpec((d, d), lambda j: (0, j)),
            pl.BlockSpec((1, d), lambda j: (0, j)),
        ],
        out_specs=pl.BlockSpec((n, d), lambda j: (0, j)),
        compiler_params=_params(1),
        name="ada",
    )(c, w_bf16, b)


def _rope_table_kernel(pos_ref, invf_ref, c_ref, s1_ref, s2_ref):
    ang = pos_ref[...] * invf_ref[...]
    r = lax.broadcasted_iota(jnp.int32, ang.shape, 1) & (DIFF_HEAD_DIM - 1)
    cos = jnp.cos(ang)
    sin = jnp.sin(ang)
    c_ref[...] = jnp.where(r < ROPE_DIM, cos, 1.0)
    s1_ref[...] = jnp.where((r >= ROPE_HALF) & (r < ROPE_DIM), sin, 0.0)
    s2_ref[...] = jnp.where(r < ROPE_HALF, -sin, 0.0)


def _rope_tables(pos):
    p = pos.shape[0]
    inv_freq = jnp.power(jnp.float32(ROPE_THETA), -jnp.arange(ROPE_HALF, dtype=F32) * (2.0 / ROPE_DIM))
    r = jnp.arange(LANES) % DIFF_HEAD_DIM
    invf_lane = jnp.where(r < ROPE_DIM, inv_freq[r % ROPE_HALF], 0.0).reshape(1, LANES).astype(F32)
    shp = jax.ShapeDtypeStruct((p, LANES), F32)
    return pl.pallas_call(
        _rope_table_kernel,
        out_shape=(shp, shp, shp),
        name="rope_tables",
    )(pos.astype(F32).reshape(p, 1), invf_lane)


def _inproj_kernel(x_ref, sh_ref, sc_ref, g_ref, w_ref, c_ref, s1_ref, s2_ref,
                   qd_ref, kdb_ref, vdb_ref, qs_ref, ksb_ref, vsb_ref, gate_ref,
                   kd4_ref, vd4_ref, ks4_ref, vs4_ref, *, sb_token_minor):
    x = x_ref[...]
    tm = x.shape[0]
    h = (_rms(x) * g_ref[...] * (1.0 + sc_ref[...]) + sh_ref[...]).astype(BF16)
    cc, s1, s2 = c_ref[...], s1_ref[...], s2_ref[...]

    def rope(u):
        return u * cc + pltpu.roll(u, ROPE_HALF, 1) * s1 + pltpu.roll(u, LANES - ROPE_HALF, 1) * s2

    o_k = DIFF_QK_W
    o_v = o_k + DIFF_QK_W
    o_sq = o_v + DIFF_V_W
    o_sk = o_sq + SB_W
    o_sv = o_sk + SB_W
    o_g = o_sv + SB_W
    for j in range(N_DIFF_HEADS):
        a = j * LANES
        qd_ref[:, a:a + LANES] = (rope(_dot(h, w_ref[:, a:a + LANES])) * QK_SCALE).astype(BF16)
        k = rope(_dot(h, w_ref[:, o_k + a:o_k + a + LANES]))
        kdb_ref[:, a:a + LANES] = k.astype(BF16)
        kd4_ref[pl.ds(j, tm, stride=N_DIFF_HEADS), :] = k
        v = _dot(h, w_ref[:, o_v + a:o_v + a + LANES])
        vdb_ref[:, a:a + LANES] = v.astype(BF16)
        vd4_ref[pl.ds(j, tm, stride=N_DIFF_HEADS), :] = v
    qs_ref[...] = (_dot(h, w_ref[:, o_sq:o_sk]) * QK_SCALE).astype(BF16)
    ks = _dot(h, w_ref[:, o_sk:o_sv])
    vs = _dot(h, w_ref[:, o_sv:o_g])
    ksb_ref[...] = ks.astype(BF16)
    vsb_ref[...] = vs.astype(BF16)
    if sb_token_minor:
        ks4_ref[...] = ks.T
        vs4_ref[...] = vs.T
    else:
        for j in range(N_SB_HEADS):
            a = j * SB_HEAD_DIM
            ks4_ref[pl.ds(j, tm, stride=N_SB_HEADS), :] = ks[:, a:a + SB_HEAD_DIM]
            vs4_ref[pl.ds(j, tm, stride=N_SB_HEADS), :] = vs[:, a:a + SB_HEAD_DIM]
    gate_ref[...] = _sigmoid(_dot(h, w_ref[:, o_g:])).astype(BF16)


def _in_proj(x, ada3, g_pre, w_in_bf16, tables, tm, blocks_per_seq, sb_token_minor):
    r, d = x.shape
    rows_per_mod = ada3.shape[1]
    n_tab_blocks = tables[0].shape[0] // tm

    def mod_spec(col):
        if rows_per_mod == 1:
            return pl.BlockSpec((None, 1, d), lambda i: (i // blocks_per_seq, 0, col))
        return pl.BlockSpec((None, tm, d), lambda i: (0, i, col))

    tab_spec = pl.BlockSpec((tm, LANES), lambda i: (i % n_tab_blocks, 0))

    def out(width, dtype, heads=1):
        return (jax.ShapeDtypeStruct((r * heads, width), dtype),
                pl.BlockSpec((tm * heads, width), lambda i: (i, 0)))

    if sb_token_minor:
        seq = tm * blocks_per_seq
        sb_out = (jax.ShapeDtypeStruct((r // seq * SB_W, seq), F32),
                  pl.BlockSpec((SB_W, tm), lambda i: (i // blocks_per_seq, i % blocks_per_seq)))
    else:
        sb_out = out(SB_HEAD_DIM, F32, N_SB_HEADS)
    outs = [out(DIFF_QK_W, BF16), out(DIFF_QK_W, BF16), out(DIFF_V_W, BF16),
            out(SB_W, BF16), out(SB_W, BF16), out(SB_W, BF16), out(2 * D_MODEL, BF16),
            out(2 * DIFF_HEAD_DIM, F32, N_DIFF_HEADS), out(DIFF_V_DIM, F32, N_DIFF_HEADS),
            sb_out, sb_out]
    return pl.pallas_call(
        functools.partial(_inproj_kernel, sb_token_minor=sb_token_minor),
        out_shape=tuple(o[0] for o in outs),
        grid=(r // tm,),
        in_specs=[
            pl.BlockSpec((tm, d), lambda i: (i, 0)),
            mod_spec(0), mod_spec(1),
            pl.BlockSpec((1, d), lambda i: (0, 0)),
            pl.BlockSpec((d, IN_PROJ_W), lambda i: (0, 0), pipeline_mode=pl.Buffered(1)),
            tab_spec, tab_spec, tab_spec,
        ],
        out_specs=tuple(o[1] for o in outs),
        compiler_params=_params(1),
        name="in_proj",
    )(x, ada3, ada3, g_pre, w_in_bf16, *tables)


def _diff_prompt_kernel(q_ref, k_ref, v_ref, lam_ref, gsub_ref, o_ref, *, tq, lam_init):
    s_len = q_ref.shape[0]
    kb = k_ref[...]
    vb = v_ref[...]
    lam = _lam_from_params(lam_ref, lam_init)
    gsub = gsub_ref[...] * (1.0 - lam_init)
    for qi in range(s_len // tq):
        kend = (qi + 1) * tq
        q = q_ref[qi * tq:kend, :]
        lane = lax.broadcasted_iota(jnp.int32, q.shape, 1)
        row = lax.broadcasted_iota(jnp.int32, (tq, kend), 0) + qi * tq
        col = lax.broadcasted_iota(jnp.int32, (tq, kend), 1)
        causal = col <= row
        kk = kb[:kend]

        def probs(qm):
            s = jnp.where(causal, _dot_nt(qm, kk), -jnp.inf)
            p = jnp.exp(s - jnp.max(s, axis=-1, keepdims=True))
            return p, jnp.sum(p, axis=-1, keepdims=True)

        p1, l1 = probs(jnp.where(lane < DIFF_HEAD_DIM, q, 0))
        p2, l2 = probs(jnp.where(lane >= DIFF_HEAD_DIM, q, 0))
        a = (p1 * (1.0 / l1) - p2 * (lam / l2)).astype(BF16)
        o = _dot(a, vb[:kend])
        o_ref[qi * tq:kend, :] = (_rms(o) * gsub).astype(BF16)


def _diff_prompt(qd, kd, vd, lam_p, g_subln, lam_init, tq):
    b, s, _ = qd.shape
    blk = lambda: pl.BlockSpec((None, s, DIFF_V_DIM), lambda i, h: (i, 0, h))
    return pl.pallas_call(
        functools.partial(_diff_prompt_kernel, tq=tq, lam_init=lam_init),
        out_shape=jax.ShapeDtypeStruct((b, s, DIFF_V_W), BF16),
        grid=(b, N_DIFF_HEADS),
        in_specs=[blk(), blk(), blk(),
                  pl.BlockSpec(lam_p.shape, lambda i, h: (0, 0)),
                  pl.BlockSpec((1, DIFF_V_DIM), lambda i, h: (0, 0))],
        out_specs=blk(),
        compiler_params=_params(2),
        name="diff_prompt",
    )(qd, kd, vd, lam_p, g_subln)


def _sb_prompt_kernel(q_ref, k_ref, v_ref, o_ref, *, tq):
    s_len = q_ref.shape[0]
    kb = k_ref[...]
    vb = v_ref[...]
    tri = _suffix_matrix()
    for qi in range(s_len // tq):
        kend = (qi + 1) * tq
        q = q_ref[qi * tq:kend, :]
        lane = lax.broadcasted_iota(jnp.int32, q.shape, 1)
        row = lax.broadcasted_iota(jnp.int32, (tq, kend), 0) + qi * tq
        col = lax.broadcasted_iota(jnp.int32, (tq, kend), 1)
        valid = col < row
        kk = kb[:kend]
        vv = vb[:kend]
        outs = []
        for half in range(2):
            in_half = (lane >= SB_HEAD_DIM) if half else (lane < SB_HEAD_DIM)
            z = _dot_nt(jnp.where(in_half, q, 0), kk)
            ls = _log_sigmoid_neg(z)
            ls_m = jnp.where(valid, ls, 0.0)
            carry = jnp.zeros((tq, 1), F32)
            after_chunks = [None] * (kend // LANES)
            for c in range(kend // LANES - 1, -1, -1):
                chunk = ls_m[:, c * LANES:(c + 1) * LANES]
                after_chunks[c] = _suffix_sum_exclusive(chunk, tri) + carry
                carry = carry + jnp.sum(chunk, axis=-1, keepdims=True)
            after = jnp.concatenate(after_chunks, axis=1)
            w = jnp.where(valid, jnp.exp(z + ls + after), 0.0).astype(BF16)
            outs.append(_dot(w, vv))
        o_ref[qi * tq:kend, :] = jnp.where(lane < SB_HEAD_DIM, outs[0], outs[1]).astype(BF16)


def _sb_prompt(qs, ks, vs, tq):
    b, s, _ = qs.shape
    blk = lambda: pl.BlockSpec((None, s, LANES), lambda i, h: (i, 0, h))
    return pl.pallas_call(
        functools.partial(_sb_prompt_kernel, tq=tq),
        out_shape=jax.ShapeDtypeStruct((b, s, SB_W), BF16),
        grid=(b, SB_W // LANES),
        in_specs=[blk(), blk(), blk()],
        out_specs=blk(),
        compiler_params=_params(2),
        name="sb_prompt",
    )(qs, ks, vs)


def _sample_attn_kernel(pt_ref, qd_ref, kdn_ref, vdn_ref, qs_ref, ksn_ref, vsn_ref,
                        ck_ref, cv_ref, sk_ref, sv_ref, lam_ref, gsub_ref,
                        od_ref, osb_ref,
                        qbd_s, m_s, l_s, acc_s, c_s, accsb_s, *, lam_init):
    del pt_ref
    p = pl.program_id(1)
    n_pages = pl.num_programs(1)
    t_new = qd_ref.shape[0]
    page = ck_ref.shape[0] // N_DIFF_HEADS
    drows = 2 * t_new
    tri = _suffix_matrix()

    def sb_q(h):
        return qs_ref[:, h * SB_HEAD_DIM:(h + 1) * SB_HEAD_DIM]

    def process(k_of, v_of, z_of, pvs_of, new):
        s = jnp.concatenate([_dot_nt(qbd_s[h * drows:(h + 1) * drows, :], k_of(h))
                             for h in range(N_DIFF_HEADS)], axis=0)
        if new:
            t_idx = lax.broadcasted_iota(jnp.int32, s.shape, 0) & (t_new - 1)
            col = lax.broadcasted_iota(jnp.int32, s.shape, 1)
            s = jnp.where(col <= t_idx, s, -jnp.inf)
        m_prev = m_s[...]
        m_new = jnp.maximum(m_prev, jnp.max(s, axis=-1, keepdims=True))
        alpha = jnp.exp(m_prev - m_new)
        pe = jnp.exp(s - m_new)
        l_s[...] = alpha * l_s[...] + jnp.sum(pe, axis=-1, keepdims=True)
        m_s[...] = m_new
        pb = pe.astype(BF16)
        pv = jnp.concatenate([_dot(pb[h * drows:(h + 1) * drows], v_of(h)) for h in range(N_DIFF_HEADS)], axis=0)
        acc_s[...] = alpha * acc_s[...] + pv
        z = jnp.concatenate([z_of(h) for h in range(N_SB_HEADS)], axis=0)
        ls = _log_sigmoid_neg(z)
        if new:
            t_idx = lax.broadcasted_iota(jnp.int32, z.shape, 0) & (t_new - 1)
            col = lax.broadcasted_iota(jnp.int32, z.shape, 1)
            valid = col < t_idx
            ls_m = jnp.where(valid, ls, 0.0)
        else:
            ls_m = ls
        after = _suffix_sum_exclusive(ls_m, tri) + c_s[...]
        w = jnp.exp(z + ls + after)
        if new:
            w = jnp.where(valid, w, 0.0)
        c_s[...] = c_s[...] + jnp.sum(ls_m, axis=-1, keepdims=True)
        wb = w.astype(BF16)
        pvs = jnp.concatenate([pvs_of(h, wb[h * t_new:(h + 1) * t_new]) for h in range(N_SB_HEADS)], axis=0)
        accsb_s[...] = accsb_s[...] + pvs

    @pl.when(p == 0)
    def _():
        row = lax.broadcasted_iota(jnp.int32, (drows, LANES), 0)
        lane = lax.broadcasted_iota(jnp.int32, (drows, LANES), 1)
        same_comp = (lane >= DIFF_HEAD_DIM) == (row >= t_new)
        for h in range(N_DIFF_HEADS):
            qh = qd_ref[:, h * LANES:(h + 1) * LANES]
            qbd_s[h * drows:(h + 1) * drows, :] = jnp.where(same_comp, jnp.concatenate([qh, qh], axis=0), 0)
        m_s[...] = jnp.full(m_s.shape, -jnp.inf, F32)
        l_s[...] = jnp.zeros(l_s.shape, F32)
        acc_s[...] = jnp.zeros(acc_s.shape, F32)
        c_s[...] = jnp.zeros(c_s.shape, F32)
        accsb_s[...] = jnp.zeros(accsb_s.shape, F32)

        def new_page(x_ref, width):
            def get(h):
                x = x_ref[:, h * width:(h + 1) * width]
                return jnp.concatenate([x, jnp.zeros((page - t_new, width), BF16)], axis=0)
            return get

        ks_new = new_page(ksn_ref, SB_HEAD_DIM)
        vs_new = new_page(vsn_ref, SB_HEAD_DIM)
        process(new_page(kdn_ref, LANES), new_page(vdn_ref, LANES),
                lambda h: _dot_nt(sb_q(h), ks_new(h)), lambda h, w: _dot(w, vs_new(h)), new=True)

    def cached_rows(x_ref):
        return lambda h: x_ref[pl.ds(h, page, stride=N_DIFF_HEADS), :].astype(BF16)

    process(cached_rows(ck_ref), cached_rows(cv_ref),
            lambda h: _dot(sb_q(h), sk_ref[h].astype(BF16)),
            lambda h, w: _dot_nt(w, sv_ref[h].astype(BF16)), new=False)

    @pl.when(p == n_pages - 1)
    def _():
        lam = _lam_from_params(lam_ref, lam_init)
        gsub = gsub_ref[...] * (1.0 - lam_init)
        acc = acc_s[...]
        inv_l = 1.0 / l_s[...]
        for h in range(N_DIFF_HEADS):
            a = acc[h * drows:(h + 1) * drows]
            il = inv_l[h * drows:(h + 1) * drows]
            o = a[:t_new] * il[:t_new] - a[t_new:] * (lam * il[t_new:])
            od_ref[:, h * DIFF_V_DIM:(h + 1) * DIFF_V_DIM] = (_rms(o) * gsub).astype(BF16)
        accsb = accsb_s[...]
        for h in range(N_SB_HEADS):
            osb_ref[:, h * SB_HEAD_DIM:(h + 1) * SB_HEAD_DIM] = accsb[h * t_new:(h + 1) * t_new].astype(BF16)


def _sample_attn(page_table, qd, kdn, vdn, qs, ksn, vsn, ck, cv, sk, sv, lam_p, g_subln, lam_init, page_base):
    db, t_new, _ = qd.shape
    n_pages = page_table.shape[1]
    page = sk.shape[3]
    assert t_new == SUBLANES and page == LANES

    def new_spec(width):
        return pl.BlockSpec((None, t_new, width), lambda b, p, pt: (b, 0, 0))

    def page_id(b, p, pt):
        return pt[b, n_pages - 1 - p] + page_base

    diff_spec = pl.BlockSpec((page * N_DIFF_HEADS, 2 * DIFF_HEAD_DIM), lambda b, p, pt: (page_id(b, p, pt), 0))
    sb_spec = pl.BlockSpec((None, N_SB_HEADS, SB_HEAD_DIM, page), lambda b, p, pt: (page_id(b, p, pt), 0, 0, 0))

    n_drow = N_DIFF_HEADS * 2 * t_new
    n_srow = N_SB_HEADS * t_new
    grid_spec = pltpu.PrefetchScalarGridSpec(
        num_scalar_prefetch=1,
        grid=(db, n_pages),
        in_specs=[new_spec(DIFF_QK_W), new_spec(DIFF_QK_W), new_spec(DIFF_V_W),
                  new_spec(SB_W), new_spec(SB_W), new_spec(SB_W),
                  diff_spec, diff_spec, sb_spec, sb_spec,
                  pl.BlockSpec(lam_p.shape, lambda b, p, pt: (0, 0)),
                  pl.BlockSpec((1, DIFF_V_DIM), lambda b, p, pt: (0, 0))],
        out_specs=(new_spec(DIFF_V_W), new_spec(SB_W)),
        scratch_shapes=[pltpu.VMEM((n_drow, 2 * DIFF_HEAD_DIM), BF16),
                        pltpu.VMEM((n_drow, 1), F32), pltpu.VMEM((n_drow, 1), F32),
                        pltpu.VMEM((n_drow, DIFF_V_DIM), F32),
                        pltpu.VMEM((n_srow, 1), F32), pltpu.VMEM((n_srow, SB_HEAD_DIM), F32)],
    )
    return pl.pallas_call(
        functools.partial(_sample_attn_kernel, lam_init=lam_init),
        out_shape=(jax.ShapeDtypeStruct((db, t_new, DIFF_V_W), BF16), jax.ShapeDtypeStruct((db, t_new, SB_W), BF16)),
        grid_spec=grid_spec,
        compiler_params=_params(2),
        name="sample_attn",
    )(page_table, qd, kdn, vdn, qs, ksn, vsn, ck, cv, sk, sv, lam_p, g_subln)


def _post_kernel(od_ref, osb_ref, gate_ref, x_ref, gt_ref, shf_ref, scf_ref,
                 wa_ref, wb_ref, wo_ref, gpost_ref, gpre_ref, x1_ref, h2_ref):
    ya = _dot(od_ref[...], wa_ref[...])
    yb = _dot(osb_ref[...], wb_ref[...])
    g = gate_ref[...].astype(F32)
    m = (g[:, :D_MODEL] * ya + g[:, D_MODEL:] * yb).astype(BF16)
    mix = _dot(m, wo_ref[...])
    x1 = x_ref[...] + gt_ref[...] * (_rms(mix) * gpost_ref[...])
    x1_ref[...] = x1
    h2_ref[...] = (_rms(x1) * gpre_ref[...] * (1.0 + scf_ref[...]) + shf_ref[...]).astype(BF16)


def _post(od, osb, gate, x, ada3, wa, wb, wo, g_post, g_pre_ffn, tm, blocks_per_seq):
    r, d = x.shape
    rows_per_mod = ada3.shape[1]

    def mod_spec(col):
        if rows_per_mod == 1:
            return pl.BlockSpec((None, 1, d), lambda i: (i // blocks_per_seq, 0, col))
        return pl.BlockSpec((None, tm, d), lambda i: (0, i, col))

    row_spec = lambda w: pl.BlockSpec((tm, w), lambda i: (i, 0))
    const = lambda a: pl.BlockSpec(a.shape, lambda i: (0, 0), pipeline_mode=pl.Buffered(1))
    return pl.pallas_call(
        _post_kernel,
        out_shape=(jax.ShapeDtypeStruct((r, d), F32), jax.ShapeDtypeStruct((r, d), BF16)),
        grid=(r // tm,),
        in_specs=[row_spec(DIFF_V_W), row_spec(SB_W), row_spec(2 * D_MODEL), row_spec(d),
                  mod_spec(2), mod_spec(3), mod_spec(4),
                  const(wa), const(wb), const(wo), const(g_post), const(g_pre_ffn)],
        out_specs=(row_spec(d), row_spec(d)),
        compiler_params=_params(1),
        name="post_attn",
    )(od, osb, gate, x, ada3, ada3, ada3, wa, wb, wo, g_post, g_pre_ffn)


def _gelu_tanh(x):
    return 0.5 * x * (1.0 + jnp.tanh(math.sqrt(2.0 / math.pi) * (x + 0.044715 * (x * x * x))))


def _ffn_tail(u, um1, um2, cw_ref, cb_ref, wd_ref, x1_ref, gt_ref, gpost_ref, y_ref):
    conv = cb_ref[...] + um2 * cw_ref[0:1, :] + um1 * cw_ref[1:2, :] + u * cw_ref[2:3, :]
    y = (_gelu_tanh(conv[:, :D_FF]) * conv[:, D_FF:]).astype(BF16)
    f = _dot(y, wd_ref[...])
    y_ref[...] = x1_ref[...] + gt_ref[...] * (_rms(f) * gpost_ref[...])


def _ffn_prompt_kernel(h2_ref, x1_ref, gt_ref, wu_ref, cw_ref, cb_ref, wd_ref, gpost_ref,
                       y_ref, st_ref, ext_s, *, blocks_per_seq):
    tm = h2_ref.shape[0]
    i = pl.program_id(0)

    @pl.when(i % blocks_per_seq == 0)
    def _():
        ext_s[0:SUBLANES, :] = jnp.zeros((SUBLANES, ext_s.shape[1]), F32)

    u = _dot(h2_ref[...], wu_ref[...])
    ext_s[SUBLANES:, :] = u
    um1 = ext_s[SUBLANES - 1:SUBLANES - 1 + tm, :]
    um2 = ext_s[SUBLANES - 2:SUBLANES - 2 + tm, :]
    _ffn_tail(u, um1, um2, cw_ref, cb_ref, wd_ref, x1_ref, gt_ref, gpost_ref, y_ref)
    st_ref[...] = ext_s[tm + SUBLANES - (CONV_WIDTH - 1):, :]
    ext_s[0:SUBLANES, :] = ext_s[tm:tm + SUBLANES, :]


def _ffn_prompt(h2, x1, ada3, wu, cw, cb, wd, g_post, tm, blocks_per_seq):
    r, d = x1.shape
    n_seq = r // (tm * blocks_per_seq)
    const = lambda a: pl.BlockSpec(a.shape, lambda i: (0, 0), pipeline_mode=pl.Buffered(1))
    row_spec = lambda w: pl.BlockSpec((tm, w), lambda i: (i, 0))
    return pl.pallas_call(
        functools.partial(_ffn_prompt_kernel, blocks_per_seq=blocks_per_seq),
        out_shape=(jax.ShapeDtypeStruct((r, d), F32),
                   jax.ShapeDtypeStruct((n_seq, CONV_WIDTH - 1, 2 * D_FF), F32)),
        grid=(r // tm,),
        in_specs=[row_spec(d), row_spec(d),
                  pl.BlockSpec((None, 1, d), lambda i: (i // blocks_per_seq, 0, 5)),
                  const(wu), const(cw), const(cb), const(wd), const(g_post)],
        out_specs=(row_spec(d),
                   pl.BlockSpec((None, CONV_WIDTH - 1, 2 * D_FF), lambda i: (i // blocks_per_seq, 0, 0))),
        scratch_shapes=[pltpu.VMEM((tm + SUBLANES, 2 * D_FF), F32)],
        compiler_params=_params(1),
        name="ffn_prompt",
    )(h2, x1, ada3, wu, cw, cb, wd, g_post)


def _ffn_sample_kernel(h2_ref, x1_ref, gt_ref, st0_ref, st1_ref, wu_ref, cw_ref, cb_ref, wd_ref, gpost_ref,
                       y_ref, u_ref, *, t_new):
    u = _dot(h2_ref[...], wu_ref[...])
    u_ref[...] = u
    t_idx = lax.broadcasted_iota(jnp.int32, u.shape, 0) & (t_new - 1)
    st0 = st0_ref[...]
    st1 = st1_ref[...]
    um1 = jnp.where(t_idx >= 1, pltpu.roll(u, 1, 0), st1)
    um2 = jnp.where(t_idx >= 2, pltpu.roll(u, 2, 0), jnp.where(t_idx == 0, st0, st1))
    _ffn_tail(u, um1, um2, cw_ref, cb_ref, wd_ref, x1_ref, gt_ref, gpost_ref, y_ref)


def _ffn_sample(h2, x1, gt_rows, st0_rows, st1_rows, wu, cw, cb, wd, g_post, t_new):
    r, d = x1.shape
    return pl.pallas_call(
        functools.partial(_ffn_sample_kernel, t_new=t_new),
        out_shape=(jax.ShapeDtypeStruct((r, d), F32), jax.ShapeDtypeStruct((r, 2 * D_FF), F32)),
        compiler_params=pltpu.CompilerParams(vmem_limit_bytes=VMEM_LIMIT_BYTES),
        name="ffn_sample",
    )(h2, x1, gt_rows, st0_rows, st1_rows, wu, cw, cb, wd, g_post)


def _pick_tile(n, target):
    t = min(n, target)
    while n % t:
        t //= 2
    return t


def _stack_layers(vals):
    return vals[0][None] if len(vals) == 1 else jnp.stack(vals, axis=0)


def kernel(x_prompt, x_sample, c_prompt, c_sample, cache_diff_k, cache_diff_v, cache_sb_k, cache_sb_v, state_conv, page_table, w_ada, b_ada, g_pre_mix, g_post_mix, g_pre_ffn, g_post_ffn, w_in, lam_q1, lam_k1, lam_q2, lam_k2, g_subln, w_br_a, w_br_b, w_out, w_up, conv_w, conv_b, w_down):
    b, s, d = x_prompt.shape
    db, t_new, _ = x_sample.shape
    depth = w_in.shape[0]
    n_phys, page = cache_diff_k.shape[1], cache_diff_k.shape[2]
    n_past = page_table.shape[1] * page
    assert d == D_MODEL and t_new == SUBLANES and page == LANES

    tm = _pick_tile(s, 256)
    tq = _pick_tile(s, 256)
    bps = s // tm
    r_s = db * t_new

    tab_p = _rope_tables(jnp.arange(s, dtype=jnp.int32))
    tab_s = tuple(jnp.tile(t, (db, 1)) for t in _rope_tables(n_past + jnp.arange(t_new, dtype=jnp.int32)))

    ck = cache_diff_k.reshape(-1, 2 * DIFF_HEAD_DIM)
    cv = cache_diff_v.reshape(-1, DIFF_V_DIM)
    sb_view = lambda c: jnp.transpose(c, (0, 1, 3, 4, 2)).reshape(depth * n_phys, N_SB_HEADS, SB_HEAD_DIM, page)
    sk = sb_view(cache_sb_k)
    sv = sb_view(cache_sb_v)

    hp = x_prompt.reshape(b * s, d)
    hs = x_sample.reshape(r_s, d)
    c_all = jnp.concatenate([c_prompt, c_sample], axis=0)
    new_p = [[] for _ in range(5)]
    new_s = [[] for _ in range(5)]
    for l in range(depth):
        lam_init = 0.8 - 0.6 * math.exp(-0.3 * l)
        bf = lambda w: w[l].astype(BF16)
        row = lambda v: v[l].reshape(1, -1)
        lam_p = jnp.stack([lam_q1[l], lam_k1[l], lam_q2[l], lam_k2[l]], axis=0)
        ada = _ada(c_all, bf(w_ada), row(b_ada))
        ada_p = ada[:b].reshape(b, 1, 6 * d)
        ada_s = jnp.repeat(ada[b:], t_new, axis=0).reshape(1, r_s, 6 * d)
        w_in_l, wa, wb, wo, wu, wd = bf(w_in), bf(w_br_a), bf(w_br_b), bf(w_out), bf(w_up), bf(w_down)
        cw, cb = conv_w[l], row(conv_b)
        gsub = row(g_subln)

        qd, kdb, vdb, qs, ksb, vsb, gate, kd4, vd4, ks4, vs4 = _in_proj(
            hp, ada_p, row(g_pre_mix), w_in_l, tab_p, tm, bps, True)
        sb_cache = lambda a: jnp.transpose(a.reshape(b, N_SB_HEADS, SB_HEAD_DIM, s), (0, 3, 1, 2))
        r3 = lambda a: a.reshape(b, s, -1)
        od = _diff_prompt(r3(qd), r3(kdb), r3(vdb), lam_p, gsub, lam_init, tq)
        osb = _sb_prompt(r3(qs), r3(ksb), r3(vsb), tq)
        x1, h2 = _post(od.reshape(b * s, -1), osb.reshape(b * s, -1), gate, hp, ada_p, wa, wb, wo,
                       row(g_post_mix), row(g_pre_ffn), tm, bps)
        hp, conv_p = _ffn_prompt(h2, x1, ada_p, wu, cw, cb, wd, row(g_post_ffn), tm, bps)
        for lst, val in zip(new_p, (kd4.reshape(b, s, N_DIFF_HEADS, 2 * DIFF_HEAD_DIM),
                                    vd4.reshape(b, s, N_DIFF_HEADS, DIFF_V_DIM),
                                    sb_cache(ks4), sb_cache(vs4), conv_p)):
            lst.append(val)

        qd, kdb, vdb, qs, ksb, vsb, gate, kd4, vd4, ks4, vs4 = _in_proj(
            hs, ada_s, row(g_pre_mix), w_in_l, tab_s, r_s, 1, False)
        r3 = lambda a: a.reshape(db, t_new, -1)
        od, osb = _sample_attn(page_table, r3(qd), r3(kdb), r3(vdb), r3(qs), r3(ksb), r3(vsb),
                               ck, cv, sk, sv, lam_p, gsub, lam_init, l * n_phys)
        x1, h2 = _post(od.reshape(r_s, -1), osb.reshape(r_s, -1), gate, hs, ada_s, wa, wb, wo,
                       row(g_post_mix), row(g_pre_ffn), r_s, 1)
        st = state_conv[l]
        hs, u_s = _ffn_sample(h2, x1, ada_s[0, :, 5 * d:], jnp.repeat(st[:, 0], t_new, axis=0),
                              jnp.repeat(st[:, 1], t_new, axis=0), wu, cw, cb, wd, row(g_post_ffn), t_new)
        for lst, val in zip(new_s, (kd4.reshape(db, t_new, N_DIFF_HEADS, 2 * DIFF_HEAD_DIM),
                                    vd4.reshape(db, t_new, N_DIFF_HEADS, DIFF_V_DIM),
                                    ks4.reshape(db, t_new, N_SB_HEADS, SB_HEAD_DIM),
                                    vs4.reshape(db, t_new, N_SB_HEADS, SB_HEAD_DIM),
                                    u_s.reshape(db, t_new, 2 * D_FF)[:, t_new - (CONV_WIDTH - 1):])):
            lst.append(val)

    outs_p = [_stack_layers(a) for a in new_p]
    outs_s = [_stack_layers(a) for a in new_s]
    return (hp.reshape(b, s, d), hs.reshape(db, t_new, d), *outs_p, *outs_s)
```

```python
import functools
import math

import jax
import jax.numpy as jnp
from jax import lax
from jax.experimental import pallas as pl
from jax.experimental.pallas import tpu as pltpu

D_MODEL = 1024
N_DIFF_HEADS = 8
DIFF_HEAD_DIM = 64
DIFF_V_DIM = 2 * DIFF_HEAD_DIM
N_SB_HEADS = 8
SB_HEAD_DIM = 64
ROPE_DIM = DIFF_HEAD_DIM // 4
ROPE_HALF = ROPE_DIM // 2
ROPE_THETA = 500000.0
D_FF = 2816
CONV_WIDTH = 3
EPS = 1e-6
DIFF_QK_W = N_DIFF_HEADS * 2 * DIFF_HEAD_DIM
DIFF_V_W = N_DIFF_HEADS * DIFF_V_DIM
SB_W = N_SB_HEADS * SB_HEAD_DIM
IN_PROJ_W = 2 * DIFF_QK_W + DIFF_V_W + 3 * SB_W + 2 * D_MODEL
LOG2E = math.log2(math.e)
Q_SCALE2 = DIFF_HEAD_DIM ** -0.5 * LOG2E

LANES = 128
SUBLANES = 8
VMEM_LIMIT_BYTES = 56 * 1024 * 1024
MAX_PAGES_PER_STEP = 8

F32 = jnp.float32
BF16 = jnp.bfloat16


def _params(n_grid_dims, vmem=VMEM_LIMIT_BYTES):
    return pltpu.CompilerParams(dimension_semantics=("arbitrary",) * n_grid_dims, vmem_limit_bytes=vmem)


def _dot(a, b):
    return jnp.dot(a, b, preferred_element_type=F32)


def _dot_nt(a, b):
    return lax.dot_general(a, b, (((1,), (1,)), ((), ())), preferred_element_type=F32)


def _sigmoid(x):
    return 1.0 / (1.0 + jnp.exp(-x))


def _rms(x):
    return x * lax.rsqrt(jnp.mean(x * x, axis=-1, keepdims=True) + EPS)


def _softplus2(z2):
    neg_abs = lax.bitcast_convert_type(lax.bitcast_convert_type(z2, jnp.uint32) | jnp.uint32(0x80000000), F32)
    return jnp.maximum(z2, 0.0) + jnp.log(1.0 + jnp.exp2(neg_abs)) * LOG2E


def _lam_from_params(lam_ref, lam_init):
    lp = lam_ref[...]
    s1 = jnp.sum(lp[0:1] * lp[1:2], axis=-1, keepdims=True)
    s2 = jnp.sum(lp[2:3] * lp[3:4], axis=-1, keepdims=True)
    return jnp.exp(s1) - jnp.exp(s2) + lam_init


def _suffix_matrix(with_total):
    width = 2 * LANES if with_total else LANES
    j = lax.broadcasted_iota(jnp.int32, (LANES, width), 0)
    s = lax.broadcasted_iota(jnp.int32, (LANES, width), 1)
    return jnp.where((j > s) | (s >= LANES), 1.0, 0.0).astype(BF16)


def _suffix_matmul(x, tri):
    hi = x.astype(BF16)
    lo = (x - hi.astype(F32)).astype(BF16)
    return _dot(hi, tri) + _dot(lo, tri)


def _ada_kernel(c_ref, w_ref, b_ref, o_ref):
    c = c_ref[...]
    a = (c * _sigmoid(c)).astype(BF16)
    o_ref[...] = _dot(a, w_ref[...]) + b_ref[...]


def _ada(c, w_bf16, b):
    n, d = c.shape
    n_out = w_bf16.shape[1]
    return pl.pallas_call(
        _ada_kernel,
        out_shape=jax.ShapeDtypeStruct((n, n_out), F32),
        grid=(n_out // d,),
        in_specs=[
            pl.BlockSpec((n, d), lambda j: (0, 0)),
            pl.BlockSpec((d, d), lambda j: (0, j)),
            pl.BlockSpec((1, d), lambda j: (0, j)),
        ],
        out_specs=pl.BlockSpec((n, d), lambda j: (0, j)),
        compiler_params=_params(1),
        name="ada",
    )(c, w_bf16, b)


def _rope_table_kernel(pos_ref, invf_ref, c_ref, s1_ref, s2_ref):
    ang = pos_ref[...] * invf_ref[...]
    r = lax.broadcasted_iota(jnp.int32, ang.shape, 1) & (DIFF_HEAD_DIM - 1)
    cos = jnp.cos(ang)
    sin = jnp.sin(ang)
    c_ref[...] = jnp.where(r < ROPE_DIM, cos, 1.0)
    s1_ref[...] = jnp.where((r >= ROPE_HALF) & (r < ROPE_DIM), sin, 0.0)
    s2_ref[...] = jnp.where(r < ROPE_HALF, -sin, 0.0)


def _rope_tables(pos):
    p = pos.shape[0]
    inv_freq = jnp.power(jnp.float32(ROPE_THETA), -jnp.arange(ROPE_HALF, dtype=F32) * (2.0 / ROPE_DIM))
    r = jnp.arange(LANES) % DIFF_HEAD_DIM
    invf_lane = jnp.where(r < ROPE_DIM, inv_freq[r % ROPE_HALF], 0.0).reshape(1, LANES).astype(F32)
    shp = jax.ShapeDtypeStruct((p, LANES), F32)
    return pl.pallas_call(
        _rope_table_kernel,
        out_shape=(shp, shp, shp),
        name="rope_tables",
    )(pos.astype(F32).reshape(p, 1), invf_lane)


def _inproj_kernel(x_ref, sh_ref, sc_ref, g_ref, w_ref, c_ref, s1_ref, s2_ref,
                   qd_ref, kdb_ref, vdb_ref, qs_ref, ksb_ref, vsb_ref, gate_ref,
                   kd4_ref, vd4_ref, ks4_ref, vs4_ref, *, sb_token_minor):
    x = x_ref[...]
    tm = x.shape[0]
    h = (_rms(x) * g_ref[...] * (1.0 + sc_ref[...]) + sh_ref[...]).astype(BF16)
    cc, s1, s2 = c_ref[...], s1_ref[...], s2_ref[...]

    def rope(u):
        return u * cc + pltpu.roll(u, ROPE_HALF, 1) * s1 + pltpu.roll(u, LANES - ROPE_HALF, 1) * s2

    o_k = DIFF_QK_W
    o_v = o_k + DIFF_QK_W
    o_sq = o_v + DIFF_V_W
    o_sk = o_sq + SB_W
    o_sv = o_sk + SB_W
    o_g = o_sv + SB_W
    for j2 in range(0, N_DIFF_HEADS, 2):
        a2 = j2 * LANES
        uq = _dot(h, w_ref[:, a2:a2 + 2 * LANES])
        uk = _dot(h, w_ref[:, o_k + a2:o_k + a2 + 2 * LANES])
        uv = _dot(h, w_ref[:, o_v + a2:o_v + a2 + 2 * LANES])
        for j in (j2, j2 + 1):
            a = j * LANES
            lo = a - a2
            qd_ref[:, a:a + LANES] = (rope(uq[:, lo:lo + LANES]) * Q_SCALE2).astype(BF16)
            k = rope(uk[:, lo:lo + LANES])
            kdb_ref[:, a:a + LANES] = k.astype(BF16)
            kd4_ref[pl.ds(j, tm, stride=N_DIFF_HEADS), :] = k
            v = uv[:, lo:lo + LANES]
            vdb_ref[:, a:a + LANES] = v.astype(BF16)
            vd4_ref[pl.ds(j, tm, stride=N_DIFF_HEADS), :] = v
    qs_ref[...] = (_dot(h, w_ref[:, o_sq:o_sk]) * Q_SCALE2).astype(BF16)
    ks = _dot(h, w_ref[:, o_sk:o_sv])
    vs = _dot(h, w_ref[:, o_sv:o_g])
    ksb_ref[...] = ks.astype(BF16)
    vsb_ref[...] = vs.astype(BF16)
    if sb_token_minor:
        ks4_ref[...] = ks.T
        vs4_ref[...] = vs.T
    else:
        for j in range(N_SB_HEADS):
            a = j * SB_HEAD_DIM
            ks4_ref[pl.ds(j, tm, stride=N_SB_HEADS), :] = ks[:, a:a + SB_HEAD_DIM]
            vs4_ref[pl.ds(j, tm, stride=N_SB_HEADS), :] = vs[:, a:a + SB_HEAD_DIM]
    gate_ref[...] = _sigmoid(_dot(h, w_ref[:, o_g:])).astype(BF16)


def _in_proj(x, ada3, g_pre, w_in_bf16, tables, tm, blocks_per_seq, sb_token_minor):
    r, d = x.shape
    rows_per_mod = ada3.shape[1]
    n_tab_blocks = tables[0].shape[0] // tm

    def mod_spec(col):
        if rows_per_mod == 1:
            return pl.BlockSpec((None, 1, d), lambda i: (i // blocks_per_seq, 0, col))
        return pl.BlockSpec((None, tm, d), lambda i: (0, i, col))

    tab_spec = pl.BlockSpec((tm, LANES), lambda i: (i % n_tab_blocks, 0))

    def out(width, dtype, heads=1):
        return (jax.ShapeDtypeStruct((r * heads, width), dtype),
                pl.BlockSpec((tm * heads, width), lambda i: (i, 0)))

    if sb_token_minor:
        seq = tm * blocks_per_seq
        sb_out = (jax.ShapeDtypeStruct((r // seq * SB_W, seq), F32),
                  pl.BlockSpec((SB_W, tm), lambda i: (i // blocks_per_seq, i % blocks_per_seq)))
    else:
        sb_out = out(SB_HEAD_DIM, F32, N_SB_HEADS)
    outs = [out(DIFF_QK_W, BF16), out(DIFF_QK_W, BF16), out(DIFF_V_W, BF16),
            out(SB_W, BF16), out(SB_W, BF16), out(SB_W, BF16), out(2 * D_MODEL, BF16),
            out(2 * DIFF_HEAD_DIM, F32, N_DIFF_HEADS), out(DIFF_V_DIM, F32, N_DIFF_HEADS),
            sb_out, sb_out]
    return pl.pallas_call(
        functools.partial(_inproj_kernel, sb_token_minor=sb_token_minor),
        out_shape=tuple(o[0] for o in outs),
        grid=(r // tm,),
        in_specs=[
            pl.BlockSpec((tm, d), lambda i: (i, 0)),
            mod_spec(0), mod_spec(1),
            pl.BlockSpec((1, d), lambda i: (0, 0)),
            pl.BlockSpec((d, IN_PROJ_W), lambda i: (0, 0), pipeline_mode=pl.Buffered(1)),
            tab_spec, tab_spec, tab_spec,
        ],
        out_specs=tuple(o[1] for o in outs),
        compiler_params=_params(1),
        name="in_proj",
    )(x, ada3, ada3, g_pre, w_in_bf16, *tables)


def _diff_prompt_kernel(q_ref, k_ref, v_ref, lam_ref, gsub_ref, o_ref, *, tq, lam_init):
    s_len = q_ref.shape[0]
    kb = k_ref[...]
    vb = v_ref[...]
    lam = _lam_from_params(lam_ref, lam_init)
    gsub = gsub_ref[...] * (1.0 - lam_init)
    for qi in range(s_len // tq):
        kend = (qi + 1) * tq
        q = q_ref[qi * tq:kend, :]
        lane = lax.broadcasted_iota(jnp.int32, q.shape, 1)
        row = lax.broadcasted_iota(jnp.int32, (tq, kend), 0) + qi * tq
        col = lax.broadcasted_iota(jnp.int32, (tq, kend), 1)
        causal = col <= row
        kk = kb[:kend]

        def probs(qm):
            s = jnp.where(causal, _dot_nt(qm, kk), -jnp.inf)
            p = jnp.exp2(s - jnp.max(s, axis=-1, keepdims=True))
            return p, jnp.sum(p, axis=-1, keepdims=True)

        p1, l1 = probs(jnp.where(lane < DIFF_HEAD_DIM, q, 0))
        p2, l2 = probs(jnp.where(lane >= DIFF_HEAD_DIM, q, 0))
        a = (p1 - p2 * (lam * l1 / l2)).astype(BF16)
        o = _dot(a, vb[:kend]) * (1.0 / l1)
        o_ref[qi * tq:kend, :] = (_rms(o) * gsub).astype(BF16)


def _diff_prompt(qd, kd, vd, lam_p, g_subln, lam_init, tq):
    b, s, _ = qd.shape
    blk = lambda: pl.BlockSpec((None, s, DIFF_V_DIM), lambda i, h: (i, 0, h))
    return pl.pallas_call(
        functools.partial(_diff_prompt_kernel, tq=tq, lam_init=lam_init),
        out_shape=jax.ShapeDtypeStruct((b, s, DIFF_V_W), BF16),
        grid=(b, N_DIFF_HEADS),
        in_specs=[blk(), blk(), blk(),
                  pl.BlockSpec(lam_p.shape, lambda i, h: (0, 0)),
                  pl.BlockSpec((1, DIFF_V_DIM), lambda i, h: (0, 0))],
        out_specs=blk(),
        compiler_params=_params(2),
        name="diff_prompt",
    )(qd, kd, vd, lam_p, g_subln)


def _sb_prompt_kernel(q_ref, k_ref, v_ref, o_ref, *, tq):
    s_len = q_ref.shape[0]
    kb = k_ref[...]
    vb = v_ref[...]
    tri = _suffix_matrix(with_total=False)
    for qi in range(s_len // tq):
        kend = (qi + 1) * tq
        q = q_ref[qi * tq:kend, :]
        lane = lax.broadcasted_iota(jnp.int32, q.shape, 1)
        row = lax.broadcasted_iota(jnp.int32, (tq, kend), 0) + qi * tq
        col = lax.broadcasted_iota(jnp.int32, (tq, kend), 1)
        valid = col < row
        kk = kb[:kend]
        vv = vb[:kend]
        outs = []
        for half in range(2):
            in_half = (lane >= SB_HEAD_DIM) if half else (lane < SB_HEAD_DIM)
            z = _dot_nt(jnp.where(in_half, q, 0), kk)
            sp = _softplus2(z)
            sp_m = jnp.where(valid, sp, 0.0)
            carry = jnp.zeros((tq, 1), F32)
            after_chunks = [None] * (kend // LANES)
            for c in range(kend // LANES - 1, -1, -1):
                chunk = sp_m[:, c * LANES:(c + 1) * LANES]
                after_chunks[c] = _suffix_matmul(chunk, tri) + carry
                carry = carry + jnp.sum(chunk, axis=-1, keepdims=True)
            after = jnp.concatenate(after_chunks, axis=1)
            w = jnp.where(valid, jnp.exp2(z - sp - after), 0.0).astype(BF16)
            outs.append(_dot(w, vv))
        o_ref[qi * tq:kend, :] = jnp.where(lane < SB_HEAD_DIM, outs[0], outs[1]).astype(BF16)


def _sb_prompt(qs, ks, vs, tq):
    b, s, _ = qs.shape
    blk = lambda: pl.BlockSpec((None, s, LANES), lambda i, h: (i, 0, h))
    return pl.pallas_call(
        functools.partial(_sb_prompt_kernel, tq=tq),
        out_shape=jax.ShapeDtypeStruct((b, s, SB_W), BF16),
        grid=(b, SB_W // LANES),
        in_specs=[blk(), blk(), blk()],
        out_specs=blk(),
        compiler_params=_params(2),
        name="sb_prompt",
    )(qs, ks, vs)


def _sample_attn_kernel(pt_ref, qd_ref, kdn_ref, vdn_ref, qs_ref, ksn_ref, vsn_ref, *refs, lam_init, group):
    del pt_ref
    ck_refs, cv_refs = refs[:group], refs[group:2 * group]
    sk_refs, sv_refs = refs[2 * group:3 * group], refs[3 * group:4 * group]
    lam_ref, gsub_ref, od_ref, osb_ref, qbd_s, m_s, l_s, acc_s, c_s, accsb_s = refs[4 * group:]
    p = pl.program_id(1)
    n_steps = pl.num_programs(1)
    t_new = qd_ref.shape[0]
    page = ck_refs[0].shape[0] // N_DIFF_HEADS
    drows = 2 * t_new
    srows = N_SB_HEADS * t_new
    tri = _suffix_matrix(with_total=True)

    def sb_q(h):
        return qs_ref[:, h * SB_HEAD_DIM:(h + 1) * SB_HEAD_DIM]

    def process(k_of, v_of, z_of, pvs_of, n_chunks, new):
        s = jnp.concatenate([_dot_nt(qbd_s[h * drows:(h + 1) * drows, :], k_of(h))
                             for h in range(N_DIFF_HEADS)], axis=0)
        if new:
            t_idx = lax.broadcasted_iota(jnp.int32, s.shape, 0) & (t_new - 1)
            col = lax.broadcasted_iota(jnp.int32, s.shape, 1)
            s = jnp.where(col <= t_idx, s, -jnp.inf)
        m_prev = m_s[...]
        m_new = jnp.maximum(m_prev, jnp.max(s, axis=-1, keepdims=True))
        alpha = jnp.exp2(m_prev - m_new)
        pe = jnp.exp2(s - m_new)
        l_s[...] = alpha * l_s[...] + jnp.sum(pe, axis=-1, keepdims=True)
        m_s[...] = m_new
        pb = pe.astype(BF16)
        pv = jnp.concatenate([_dot(pb[h * drows:(h + 1) * drows], v_of(h)) for h in range(N_DIFF_HEADS)], axis=0)
        acc_s[...] = alpha * acc_s[...] + pv
        z = jnp.concatenate([z_of(h) for h in range(N_SB_HEADS)], axis=0)
        sp = _softplus2(z)
        if new:
            t_idx = lax.broadcasted_iota(jnp.int32, z.shape, 0) & (t_new - 1)
            col = lax.broadcasted_iota(jnp.int32, z.shape, 1)
            valid = col < t_idx
            sp_m = jnp.where(valid, sp, 0.0)
        else:
            sp_m = sp
        stacked = jnp.concatenate([sp_m[:, j * page:(j + 1) * page] for j in range(n_chunks)], axis=0)
        both = _suffix_matmul(stacked, tri)
        suffix, total = both[:, :LANES], both[:, LANES:]
        carry = c_s[...]
        after = []
        for j in range(n_chunks):
            after.append(suffix[j * srows:(j + 1) * srows] + carry)
            carry = carry + total[j * srows:(j + 1) * srows]
        c_s[...] = carry
        w = jnp.exp2(z - sp - jnp.concatenate(after, axis=1))
        if new:
            w = jnp.where(valid, w, 0.0)
        wb = w.astype(BF16)
        pvs = jnp.concatenate([pvs_of(h, wb[h * t_new:(h + 1) * t_new]) for h in range(N_SB_HEADS)], axis=0)
        accsb_s[...] = accsb_s[...] + pvs

    @pl.when(p == 0)
    def _():
        row = lax.broadcasted_iota(jnp.int32, (drows, LANES), 0)
        lane = lax.broadcasted_iota(jnp.int32, (drows, LANES), 1)
        same_comp = (lane >= DIFF_HEAD_DIM) == (row >= t_new)
        for h in range(N_DIFF_HEADS):
            qh = qd_ref[:, h * LANES:(h + 1) * LANES]
            qbd_s[h * drows:(h + 1) * drows, :] = jnp.where(same_comp, jnp.concatenate([qh, qh], axis=0), 0)
        m_s[...] = jnp.full(m_s.shape, -jnp.inf, F32)
        l_s[...] = jnp.zeros(l_s.shape, F32)
        acc_s[...] = jnp.zeros(acc_s.shape, F32)
        c_s[...] = jnp.zeros(c_s.shape, F32)
        accsb_s[...] = jnp.zeros(accsb_s.shape, F32)

        def new_page(x_ref, width):
            def get(h):
                x = x_ref[:, h * width:(h + 1) * width]
                return jnp.concatenate([x, jnp.zeros((page - t_new, width), BF16)], axis=0)
            return get

        ks_new = new_page(ksn_ref, SB_HEAD_DIM)
        vs_new = new_page(vsn_ref, SB_HEAD_DIM)
        process(new_page(kdn_ref, LANES), new_page(vdn_ref, LANES),
                lambda h: _dot_nt(sb_q(h), ks_new(h)), lambda h, w: _dot(w, vs_new(h)), 1, new=True)

    def cached_rows(x_refs):
        return lambda h: jnp.concatenate(
            [r[pl.ds(h, page, stride=N_DIFF_HEADS), :].astype(BF16) for r in x_refs], axis=0)

    def cached_cols(x_refs):
        return lambda h: jnp.concatenate([r[h].astype(BF16) for r in x_refs], axis=1)

    sk_of, sv_of = cached_cols(sk_refs), cached_cols(sv_refs)
    process(cached_rows(ck_refs), cached_rows(cv_refs),
            lambda h: _dot(sb_q(h), sk_of(h)), lambda h, w: _dot_nt(w, sv_of(h)), group, new=False)

    @pl.when(p == n_steps - 1)
    def _():
        lam = _lam_from_params(lam_ref, lam_init)
        gsub = gsub_ref[...] * (1.0 - lam_init)
        acc = acc_s[...]
        inv_l = 1.0 / l_s[...]
        for h in range(N_DIFF_HEADS):
            a = acc[h * drows:(h + 1) * drows]
            il = inv_l[h * drows:(h + 1) * drows]
            o = a[:t_new] * il[:t_new] - a[t_new:] * (lam * il[t_new:])
            od_ref[:, h * DIFF_V_DIM:(h + 1) * DIFF_V_DIM] = (_rms(o) * gsub).astype(BF16)
        accsb = accsb_s[...]
        for h in range(N_SB_HEADS):
            osb_ref[:, h * SB_HEAD_DIM:(h + 1) * SB_HEAD_DIM] = accsb[h * t_new:(h + 1) * t_new].astype(BF16)


def _sample_attn(page_table, qd, kdn, vdn, qs, ksn, vsn, ck, cv, sk, sv, lam_p, g_subln, lam_init, page_base):
    db, t_new, _ = qd.shape
    n_pages = page_table.shape[1]
    page = sk.shape[3]
    assert t_new == SUBLANES and page == LANES
    group = max(g for g in range(1, MAX_PAGES_PER_STEP + 1) if n_pages % g == 0)

    def new_spec(width):
        return pl.BlockSpec((None, t_new, width), lambda b, p, pt: (b, 0, 0))

    def page_id(b, p, pt, j):
        return pt[b, n_pages - 1 - (p * group + j)] + page_base

    def diff_spec(j):
        return pl.BlockSpec((page * N_DIFF_HEADS, 2 * DIFF_HEAD_DIM), lambda b, p, pt: (page_id(b, p, pt, j), 0))

    def sb_spec(j):
        return pl.BlockSpec((None, N_SB_HEADS, SB_HEAD_DIM, page), lambda b, p, pt: (page_id(b, p, pt, j), 0, 0, 0))

    pages = range(group)
    n_drow = N_DIFF_HEADS * 2 * t_new
    n_srow = N_SB_HEADS * t_new
    grid_spec = pltpu.PrefetchScalarGridSpec(
        num_scalar_prefetch=1,
        grid=(db, n_pages // group),
        in_specs=[new_spec(DIFF_QK_W), new_spec(DIFF_QK_W), new_spec(DIFF_V_W),
                  new_spec(SB_W), new_spec(SB_W), new_spec(SB_W),
                  *[diff_spec(j) for j in pages], *[diff_spec(j) for j in pages],
                  *[sb_spec(j) for j in pages], *[sb_spec(j) for j in pages],
                  pl.BlockSpec(lam_p.shape, lambda b, p, pt: (0, 0)),
                  pl.BlockSpec((1, DIFF_V_DIM), lambda b, p, pt: (0, 0))],
        out_specs=(new_spec(DIFF_V_W), new_spec(SB_W)),
        scratch_shapes=[pltpu.VMEM((n_drow, 2 * DIFF_HEAD_DIM), BF16),
                        pltpu.VMEM((n_drow, 1), F32), pltpu.VMEM((n_drow, 1), F32),
                        pltpu.VMEM((n_drow, DIFF_V_DIM), F32),
                        pltpu.VMEM((n_srow, LANES), F32), pltpu.VMEM((n_srow, SB_HEAD_DIM), F32)],
    )
    return pl.pallas_call(
        functools.partial(_sample_attn_kernel, lam_init=lam_init, group=group),
        out_shape=(jax.ShapeDtypeStruct((db, t_new, DIFF_V_W), BF16), jax.ShapeDtypeStruct((db, t_new, SB_W), BF16)),
        grid_spec=grid_spec,
        compiler_params=_params(2),
        name="sample_attn",
    )(page_table, qd, kdn, vdn, qs, ksn, vsn, *[ck] * group, *[cv] * group, *[sk] * group, *[sv] * group,
      lam_p, g_subln)


def _post_kernel(od_ref, osb_ref, gate_ref, x_ref, gt_ref, shf_ref, scf_ref,
                 wa_ref, wb_ref, wo_ref, gpost_ref, gpre_ref, x1_ref, h2_ref):
    ya = _dot(od_ref[...], wa_ref[...])
    yb = _dot(osb_ref[...], wb_ref[...])
    g = gate_ref[...].astype(F32)
    m = (g[:, :D_MODEL] * ya + g[:, D_MODEL:] * yb).astype(BF16)
    mix = _dot(m, wo_ref[...])
    x1 = x_ref[...] + gt_ref[...] * (_rms(mix) * gpost_ref[...])
    x1_ref[...] = x1
    h2_ref[...] = (_rms(x1) * gpre_ref[...] * (1.0 + scf_ref[...]) + shf_ref[...]).astype(BF16)


def _post(od, osb, gate, x, ada3, wa, wb, wo, g_post, g_pre_ffn, tm, blocks_per_seq):
    r, d = x.shape
    rows_per_mod = ada3.shape[1]

    def mod_spec(col):
        if rows_per_mod == 1:
            return pl.BlockSpec((None, 1, d), lambda i: (i // blocks_per_seq, 0, col))
        return pl.BlockSpec((None, tm, d), lambda i: (0, i, col))

    row_spec = lambda w: pl.BlockSpec((tm, w), lambda i: (i, 0))
    const = lambda a: pl.BlockSpec(a.shape, lambda i: (0, 0), pipeline_mode=pl.Buffered(1))
    return pl.pallas_call(
        _post_kernel,
        out_shape=(jax.ShapeDtypeStruct((r, d), F32), jax.ShapeDtypeStruct((r, d), BF16)),
        grid=(r // tm,),
        in_specs=[row_spec(DIFF_V_W), row_spec(SB_W), row_spec(2 * D_MODEL), row_spec(d),
                  mod_spec(2), mod_spec(3), mod_spec(4),
                  const(wa), const(wb), const(wo), const(g_post), const(g_pre_ffn)],
        out_specs=(row_spec(d), row_spec(d)),
        compiler_params=_params(1),
        name="post_attn",
    )(od, osb, gate, x, ada3, ada3, ada3, wa, wb, wo, g_post, g_pre_ffn)


def _gelu_tanh(x):
    return 0.5 * x * (1.0 + jnp.tanh(math.sqrt(2.0 / math.pi) * (x + 0.044715 * (x * x * x))))


def _ffn_tail(u, um1, um2, cw_ref, cb_ref, wd_ref, x1_ref, gt_ref, gpost_ref, y_ref):
    conv = cb_ref[...] + um2 * cw_ref[0:1, :] + um1 * cw_ref[1:2, :] + u * cw_ref[2:3, :]
    y = (_gelu_tanh(conv[:, :D_FF]) * conv[:, D_FF:]).astype(BF16)
    f = _dot(y, wd_ref[...])
    y_ref[...] = x1_ref[...] + gt_ref[...] * (_rms(f) * gpost_ref[...])


def _ffn_prompt_kernel(h2_ref, x1_ref, gt_ref, wu_ref, cw_ref, cb_ref, wd_ref, gpost_ref,
                       y_ref, st_ref, ext_s, *, blocks_per_seq):
    tm = h2_ref.shape[0]
    i = pl.program_id(0)

    @pl.when(i % blocks_per_seq == 0)
    def _():
        ext_s[0:SUBLANES, :] = jnp.zeros((SUBLANES, ext_s.shape[1]), F32)

    u = _dot(h2_ref[...], wu_ref[...])
    ext_s[SUBLANES:, :] = u
    um1 = ext_s[SUBLANES - 1:SUBLANES - 1 + tm, :]
    um2 = ext_s[SUBLANES - 2:SUBLANES - 2 + tm, :]
    _ffn_tail(u, um1, um2, cw_ref, cb_ref, wd_ref, x1_ref, gt_ref, gpost_ref, y_ref)
    st_ref[...] = ext_s[tm + SUBLANES - (CONV_WIDTH - 1):, :]
    ext_s[0:SUBLANES, :] = ext_s[tm:tm + SUBLANES, :]


def _ffn_prompt(h2, x1, ada3, wu, cw, cb, wd, g_post, tm, blocks_per_seq):
    r, d = x1.shape
    n_seq = r // (tm * blocks_per_seq)
    const = lambda a: pl.BlockSpec(a.shape, lambda i: (0, 0), pipeline_mode=pl.Buffered(1))
    row_spec = lambda w: pl.BlockSpec((tm, w), lambda i: (i, 0))
    return pl.pallas_call(
        functools.partial(_ffn_prompt_kernel, blocks_per_seq=blocks_per_seq),
        out_shape=(jax.ShapeDtypeStruct((r, d), F32),
                   jax.ShapeDtypeStruct((n_seq, CONV_WIDTH - 1, 2 * D_FF), F32)),
        grid=(r // tm,),
        in_specs=[row_spec(d), row_spec(d),
                  pl.BlockSpec((None, 1, d), lambda i: (i // blocks_per_seq, 0, 5)),
                  const(wu), const(cw), const(cb), const(wd), const(g_post)],
        out_specs=(row_spec(d),
                   pl.BlockSpec((None, CONV_WIDTH - 1, 2 * D_FF), lambda i: (i // blocks_per_seq, 0, 0))),
        scratch_shapes=[pltpu.VMEM((tm + SUBLANES, 2 * D_FF), F32)],
        compiler_params=_params(1),
        name="ffn_prompt",
    )(h2, x1, ada3, wu, cw, cb, wd, g_post)


def _ffn_sample_kernel(h2_ref, x1_ref, gt_ref, st0_ref, st1_ref, wu_ref, cw_ref, cb_ref, wd_ref, gpost_ref,
                       y_ref, u_ref, *, t_new):
    u = _dot(h2_ref[...], wu_ref[...])
    u_ref[...] = u
    t_idx = lax.broadcasted_iota(jnp.int32, u.shape, 0) & (t_new - 1)
    st0 = st0_ref[...]
    st1 = st1_ref[...]
    um1 = jnp.where(t_idx >= 1, pltpu.roll(u, 1, 0), st1)
    um2 = jnp.where(t_idx >= 2, pltpu.roll(u, 2, 0), jnp.where(t_idx == 0, st0, st1))
    _ffn_tail(u, um1, um2, cw_ref, cb_ref, wd_ref, x1_ref, gt_ref, gpost_ref, y_ref)


def _ffn_sample(h2, x1, gt_rows, st0_rows, st1_rows, wu, cw, cb, wd, g_post, t_new):
    r, d = x1.shape
    return pl.pallas_call(
        functools.partial(_ffn_sample_kernel, t_new=t_new),
        out_shape=(jax.ShapeDtypeStruct((r, d), F32), jax.ShapeDtypeStruct((r, 2 * D_FF), F32)),
        compiler_params=pltpu.CompilerParams(vmem_limit_bytes=VMEM_LIMIT_BYTES),
        name="ffn_sample",
    )(h2, x1, gt_rows, st0_rows, st1_rows, wu, cw, cb, wd, g_post)


def _pick_tile(n, target):
    t = min(n, target)
    while n % t:
        t //= 2
    return t


def _stack_layers(vals):
    return vals[0][None] if len(vals) == 1 else jnp.stack(vals, axis=0)


def kernel(x_prompt, x_sample, c_prompt, c_sample, cache_diff_k, cache_diff_v, cache_sb_k, cache_sb_v, state_conv, page_table, w_ada, b_ada, g_pre_mix, g_post_mix, g_pre_ffn, g_post_ffn, w_in, lam_q1, lam_k1, lam_q2, lam_k2, g_subln, w_br_a, w_br_b, w_out, w_up, conv_w, conv_b, w_down):
    b, s, d = x_prompt.shape
    db, t_new, _ = x_sample.shape
    depth = w_in.shape[0]
    n_phys, page = cache_diff_k.shape[1], cache_diff_k.shape[2]
    n_past = page_table.shape[1] * page
    assert d == D_MODEL and t_new == SUBLANES and page == LANES

    tm = _pick_tile(s, 256)
    tq = _pick_tile(s, 256)
    bps = s // tm
    r_s = db * t_new

    tab_p = _rope_tables(jnp.arange(s, dtype=jnp.int32))
    tab_s = tuple(jnp.tile(t, (db, 1)) for t in _rope_tables(n_past + jnp.arange(t_new, dtype=jnp.int32)))

    ck = cache_diff_k.reshape(-1, 2 * DIFF_HEAD_DIM)
    cv = cache_diff_v.reshape(-1, DIFF_V_DIM)
    sb_view = lambda c: jnp.transpose(c, (0, 1, 3, 4, 2)).reshape(depth * n_phys, N_SB_HEADS, SB_HEAD_DIM, page)
    sk = sb_view(cache_sb_k)
    sv = sb_view(cache_sb_v)

    hp = x_prompt.reshape(b * s, d)
    hs = x_sample.reshape(r_s, d)
    c_all = jnp.concatenate([c_prompt, c_sample], axis=0)
    new_p = [[] for _ in range(5)]
    new_s = [[] for _ in range(5)]
    for l in range(depth):
        lam_init = 0.8 - 0.6 * math.exp(-0.3 * l)
        bf = lambda w: w[l].astype(BF16)
        row = lambda v: v[l].reshape(1, -1)
        lam_p = jnp.stack([lam_q1[l], lam_k1[l], lam_q2[l], lam_k2[l]], axis=0)
        ada = _ada(c_all, bf(w_ada), row(b_ada))
        ada_p = ada[:b].reshape(b, 1, 6 * d)
        ada_s = jnp.repeat(ada[b:], t_new, axis=0).reshape(1, r_s, 6 * d)
        w_in_l, wa, wb, wo, wu, wd = bf(w_in), bf(w_br_a), bf(w_br_b), bf(w_out), bf(w_up), bf(w_down)
        cw, cb = conv_w[l], row(conv_b)
        gsub = row(g_subln)

        qd, kdb, vdb, qs, ksb, vsb, gate, kd4, vd4, ks4, vs4 = _in_proj(
            hp, ada_p, row(g_pre_mix), w_in_l, tab_p, tm, bps, True)
        sb_cache = lambda a: jnp.transpose(a.reshape(b, N_SB_HEADS, SB_HEAD_DIM, s), (0, 3, 1, 2))
        r3 = lambda a: a.reshape(b, s, -1)
        od = _diff_prompt(r3(qd), r3(kdb), r3(vdb), lam_p, gsub, lam_init, tq)
        osb = _sb_prompt(r3(qs), r3(ksb), r3(vsb), tq)
        x1, h2 = _post(od.reshape(b * s, -1), osb.reshape(b * s, -1), gate, hp, ada_p, wa, wb, wo,
                       row(g_post_mix), row(g_pre_ffn), tm, bps)
        hp, conv_p = _ffn_prompt(h2, x1, ada_p, wu, cw, cb, wd, row(g_post_ffn), tm, bps)
        for lst, val in zip(new_p, (kd4.reshape(b, s, N_DIFF_HEADS, 2 * DIFF_HEAD_DIM),
                                    vd4.reshape(b, s, N_DIFF_HEADS, DIFF_V_DIM),
                                    sb_cache(ks4), sb_cache(vs4), conv_p)):
            lst.append(val)

        qd, kdb, vdb, qs, ksb, vsb, gate, kd4, vd4, ks4, vs4 = _in_proj(
            hs, ada_s, row(g_pre_mix), w_in_l, tab_s, r_s, 1, False)
        r3 = lambda a: a.reshape(db, t_new, -1)
        od, osb = _sample_attn(page_table, r3(qd), r3(kdb), r3(vdb), r3(qs), r3(ksb), r3(vsb),
                               ck, cv, sk, sv, lam_p, gsub, lam_init, l * n_phys)
        x1, h2 = _post(od.reshape(r_s, -1), osb.reshape(r_s, -1), gate, hs, ada_s, wa, wb, wo,
                       row(g_post_mix), row(g_pre_ffn), r_s, 1)
        st = state_conv[l]
        hs, u_s = _ffn_sample(h2, x1, ada_s[0, :, 5 * d:], jnp.repeat(st[:, 0], t_new, axis=0),
                              jnp.repeat(st[:, 1], t_new, axis=0), wu, cw, cb, wd, row(g_post_ffn), t_new)
        for lst, val in zip(new_s, (kd4.reshape(db, t_new, N_DIFF_HEADS, 2 * DIFF_HEAD_DIM),
                                    vd4.reshape(db, t_new, N_DIFF_HEADS, DIFF_V_DIM),
                                    ks4.reshape(db, t_new, N_SB_HEADS, SB_HEAD_DIM),
                                    vs4.reshape(db, t_new, N_SB_HEADS, SB_HEAD_DIM),
                                    u_s.reshape(db, t_new, 2 * D_FF)[:, t_new - (CONV_WIDTH - 1):])):
            lst.append(val)

    outs_p = [_stack_layers(a) for a in new_p]
    outs_s = [_stack_layers(a) for a in new_s]
    return (hp.reshape(b, s, d), hs.reshape(db, t_new, d), *outs_p, *outs_s)
```

```python
import functools
import math

import jax
import jax.numpy as jnp
from jax import lax
from jax.experimental import pallas as pl
from jax.experimental.pallas import tpu as pltpu

D_MODEL = 1024
N_DIFF_HEADS = 8
DIFF_HEAD_DIM = 64
DIFF_V_DIM = 2 * DIFF_HEAD_DIM
N_SB_HEADS = 8
SB_HEAD_DIM = 64
ROPE_DIM = DIFF_HEAD_DIM // 4
ROPE_HALF = ROPE_DIM // 2
ROPE_THETA = 500000.0
D_FF = 2816
CONV_WIDTH = 3
EPS = 1e-6
DIFF_QK_W = N_DIFF_HEADS * 2 * DIFF_HEAD_DIM
DIFF_V_W = N_DIFF_HEADS * DIFF_V_DIM
SB_W = N_SB_HEADS * SB_HEAD_DIM
IN_PROJ_W = 2 * DIFF_QK_W + DIFF_V_W + 3 * SB_W + 2 * D_MODEL
LOG2E = math.log2(math.e)
Q_SCALE2 = DIFF_HEAD_DIM ** -0.5 * LOG2E
SOFTPLUS2_CLAMP = 64.0

LANES = 128
SUBLANES = 8
VMEM_LIMIT_BYTES = 56 * 1024 * 1024
MAX_PAGES_PER_STEP = 8

F32 = jnp.float32
BF16 = jnp.bfloat16


def _params(n_grid_dims, vmem=VMEM_LIMIT_BYTES):
    return pltpu.CompilerParams(dimension_semantics=("arbitrary",) * n_grid_dims, vmem_limit_bytes=vmem)


def _dot(a, b):
    return jnp.dot(a, b, preferred_element_type=F32)


def _dot_nt(a, b):
    return lax.dot_general(a, b, (((1,), (1,)), ((), ())), preferred_element_type=F32)


def _sigmoid(x):
    return 1.0 / (1.0 + jnp.exp(-x))


def _rms(x):
    return x * lax.rsqrt(jnp.mean(x * x, axis=-1, keepdims=True) + EPS)


def _softplus2(z2):
    return jnp.maximum(jnp.log(1.0 + jnp.exp2(jnp.minimum(z2, SOFTPLUS2_CLAMP))) * LOG2E, z2)


def _lam_from_params(lam_ref, lam_init):
    lp = lam_ref[...]
    s1 = jnp.sum(lp[0:1] * lp[1:2], axis=-1, keepdims=True)
    s2 = jnp.sum(lp[2:3] * lp[3:4], axis=-1, keepdims=True)
    return jnp.exp(s1) - jnp.exp(s2) + lam_init


def _suffix_matrix(with_total):
    width = 2 * LANES if with_total else LANES
    j = lax.broadcasted_iota(jnp.int32, (2 * LANES, width), 0) & (LANES - 1)
    s = lax.broadcasted_iota(jnp.int32, (2 * LANES, width), 1)
    return jnp.where((j > s) | (s >= LANES), 1.0, 0.0).astype(BF16)


def _suffix_matmul(x, tri):
    hi = x.astype(BF16)
    lo = (x - hi.astype(F32)).astype(BF16)
    return _dot(jnp.concatenate([hi, lo], axis=1), tri)


def _ada_kernel(c_ref, w_ref, b_ref, o_ref):
    c = c_ref[...]
    a = (c * _sigmoid(c)).astype(BF16)
    o_ref[...] = _dot(a, w_ref[...]) + b_ref[...]


def _ada(c, w_bf16, b):
    n, d = c.shape
    n_out = w_bf16.shape[1]
    return pl.pallas_call(
        _ada_kernel,
        out_shape=jax.ShapeDtypeStruct((n, n_out), F32),
        grid=(n_out // d,),
        in_specs=[
            pl.BlockSpec((n, d), lambda j: (0, 0)),
            pl.BlockSpec((d, d), lambda j: (0, j)),
            pl.BlockSpec((1, d), lambda j: (0, j)),
        ],
        out_specs=pl.BlockSpec((n, d), lambda j: (0, j)),
        compiler_params=_params(1),
        name="ada",
    )(c, w_bf16, b)


def _rope_table_kernel(pos_ref, invf_ref, c_ref, s1_ref, s2_ref):
    ang = pos_ref[...] * invf_ref[...]
    r = lax.broadcasted_iota(jnp.int32, ang.shape, 1) & (DIFF_HEAD_DIM - 1)
    cos = jnp.cos(ang)
    sin = jnp.sin(ang)
    c_ref[...] = jnp.where(r < ROPE_DIM, cos, 1.0)
    s1_ref[...] = jnp.where((r >= ROPE_HALF) & (r < ROPE_DIM), sin, 0.0)
    s2_ref[...] = jnp.where(r < ROPE_HALF, -sin, 0.0)


def _rope_tables(pos):
    p = pos.shape[0]
    inv_freq = jnp.power(jnp.float32(ROPE_THETA), -jnp.arange(ROPE_HALF, dtype=F32) * (2.0 / ROPE_DIM))
    r = jnp.arange(LANES) % DIFF_HEAD_DIM
    invf_lane = jnp.where(r < ROPE_DIM, inv_freq[r % ROPE_HALF], 0.0).reshape(1, LANES).astype(F32)
    shp = jax.ShapeDtypeStruct((p, LANES), F32)
    return pl.pallas_call(
        _rope_table_kernel,
        out_shape=(shp, shp, shp),
        name="rope_tables",
    )(pos.astype(F32).reshape(p, 1), invf_lane)


def _inproj_kernel(x_ref, sh_ref, sc_ref, g_ref, w_ref, c_ref, s1_ref, s2_ref,
                   qd_ref, kdb_ref, vdb_ref, qs_ref, ksb_ref, vsb_ref, gate_ref,
                   kd4_ref, vd4_ref, ks4_ref, vs4_ref, *, sb_token_minor):
    x = x_ref[...]
    tm = x.shape[0]
    h = (_rms(x) * g_ref[...] * (1.0 + sc_ref[...]) + sh_ref[...]).astype(BF16)
    cc, s1, s2 = c_ref[...], s1_ref[...], s2_ref[...]

    def rope(u):
        return u * cc + pltpu.roll(u, ROPE_HALF, 1) * s1 + pltpu.roll(u, LANES - ROPE_HALF, 1) * s2

    o_k = DIFF_QK_W
    o_v = o_k + DIFF_QK_W
    o_sq = o_v + DIFF_V_W
    o_sk = o_sq + SB_W
    o_sv = o_sk + SB_W
    o_g = o_sv + SB_W
    for j2 in range(0, N_DIFF_HEADS, 2):
        a2 = j2 * LANES
        uq = _dot(h, w_ref[:, a2:a2 + 2 * LANES])
        uk = _dot(h, w_ref[:, o_k + a2:o_k + a2 + 2 * LANES])
        uv = _dot(h, w_ref[:, o_v + a2:o_v + a2 + 2 * LANES])
        for j in (j2, j2 + 1):
            a = j * LANES
            lo = a - a2
            qd_ref[:, a:a + LANES] = (rope(uq[:, lo:lo + LANES]) * Q_SCALE2).astype(BF16)
            k = rope(uk[:, lo:lo + LANES])
            kdb_ref[:, a:a + LANES] = k.astype(BF16)
            kd4_ref[pl.ds(j, tm, stride=N_DIFF_HEADS), :] = k
            v = uv[:, lo:lo + LANES]
            vdb_ref[:, a:a + LANES] = v.astype(BF16)
            vd4_ref[pl.ds(j, tm, stride=N_DIFF_HEADS), :] = v
    qs_ref[...] = (_dot(h, w_ref[:, o_sq:o_sk]) * Q_SCALE2).astype(BF16)
    ks = _dot(h, w_ref[:, o_sk:o_sv])
    vs = _dot(h, w_ref[:, o_sv:o_g])
    ksb_ref[...] = ks.astype(BF16)
    vsb_ref[...] = vs.astype(BF16)
    if sb_token_minor:
        ks4_ref[...] = ks.T
        vs4_ref[...] = vs.T
    else:
        for j in range(N_SB_HEADS):
            a = j * SB_HEAD_DIM
            ks4_ref[pl.ds(j, tm, stride=N_SB_HEADS), :] = ks[:, a:a + SB_HEAD_DIM]
            vs4_ref[pl.ds(j, tm, stride=N_SB_HEADS), :] = vs[:, a:a + SB_HEAD_DIM]
    gate_ref[...] = _sigmoid(_dot(h, w_ref[:, o_g:])).astype(BF16)


def _in_proj(x, ada3, g_pre, w_in_bf16, tables, tm, blocks_per_seq, sb_token_minor):
    r, d = x.shape
    rows_per_mod = ada3.shape[1]
    n_tab_blocks = tables[0].shape[0] // tm

    def mod_spec(col):
        if rows_per_mod == 1:
            return pl.BlockSpec((None, 1, d), lambda i: (i // blocks_per_seq, 0, col))
        return pl.BlockSpec((None, tm, d), lambda i: (0, i, col))

    tab_spec = pl.BlockSpec((tm, LANES), lambda i: (i % n_tab_blocks, 0))

    def out(width, dtype, heads=1):
        return (jax.ShapeDtypeStruct((r * heads, width), dtype),
                pl.BlockSpec((tm * heads, width), lambda i: (i, 0)))

    if sb_token_minor:
        seq = tm * blocks_per_seq
        sb_out = (jax.ShapeDtypeStruct((r // seq * SB_W, seq), F32),
                  pl.BlockSpec((SB_W, tm), lambda i: (i // blocks_per_seq, i % blocks_per_seq)))
    else:
        sb_out = out(SB_HEAD_DIM, F32, N_SB_HEADS)
    outs = [out(DIFF_QK_W, BF16), out(DIFF_QK_W, BF16), out(DIFF_V_W, BF16),
            out(SB_W, BF16), out(SB_W, BF16), out(SB_W, BF16), out(2 * D_MODEL, BF16),
            out(2 * DIFF_HEAD_DIM, F32, N_DIFF_HEADS), out(DIFF_V_DIM, F32, N_DIFF_HEADS),
            sb_out, sb_out]
    return pl.pallas_call(
        functools.partial(_inproj_kernel, sb_token_minor=sb_token_minor),
        out_shape=tuple(o[0] for o in outs),
        grid=(r // tm,),
        in_specs=[
            pl.BlockSpec((tm, d), lambda i: (i, 0)),
            mod_spec(0), mod_spec(1),
            pl.BlockSpec((1, d), lambda i: (0, 0)),
            pl.BlockSpec((d, IN_PROJ_W), lambda i: (0, 0), pipeline_mode=pl.Buffered(1)),
            tab_spec, tab_spec, tab_spec,
        ],
        out_specs=tuple(o[1] for o in outs),
        compiler_params=_params(1),
        name="in_proj",
    )(x, ada3, ada3, g_pre, w_in_bf16, *tables)


def _split_halves(q):
    lane = lax.broadcasted_iota(jnp.int32, q.shape, 1)
    return jnp.concatenate([jnp.where(lane < DIFF_HEAD_DIM, q, 0), jnp.where(lane >= DIFF_HEAD_DIM, q, 0)], axis=0)


def _block_mask(shape, tq, q0, c0, strict):
    row = (lax.broadcasted_iota(jnp.int32, shape, 0) & (tq - 1)) + q0
    col = lax.broadcasted_iota(jnp.int32, shape, 1) + c0
    return (col < row) if strict else (col <= row)


def _diff_prompt_kernel(q_ref, k_ref, v_ref, lam_ref, gsub_ref, o_ref, *, tq, tk, lam_init):
    s_len = q_ref.shape[0]
    kb = k_ref[...]
    vaug = jnp.concatenate([v_ref[...], jnp.ones((s_len, LANES), BF16)], axis=1)
    lam = _lam_from_params(lam_ref, lam_init)
    gsub = gsub_ref[...] * (1.0 - lam_init)
    for qi in range(s_len // tq):
        q0 = qi * tq
        kend = q0 + tq
        qq = _split_halves(q_ref[q0:kend, :])
        m = jnp.full((2 * tq, 1), -jnp.inf, F32)
        acc = jnp.zeros((2 * tq, 2 * LANES), F32)
        for c0 in range(0, kend, tk):
            c1 = min(c0 + tk, kend)
            s = _dot_nt(qq, kb[c0:c1])
            if c1 > q0:
                s = jnp.where(_block_mask(s.shape, tq, q0, c0, strict=False), s, -jnp.inf)
            m_new = jnp.maximum(m, jnp.max(s, axis=-1, keepdims=True))
            p = jnp.exp2(s - m_new).astype(BF16)
            acc = jnp.exp2(m - m_new) * acc + _dot(p, vaug[c0:c1])
            m = m_new
        inv_l = 1.0 / acc[:, LANES:LANES + 1]
        o = acc[:tq, :LANES] * inv_l[:tq] - acc[tq:, :LANES] * (lam * inv_l[tq:])
        o_ref[q0:kend, :] = (_rms(o) * gsub).astype(BF16)


def _diff_prompt(qd, kd, vd, lam_p, g_subln, lam_init, tq):
    b, s, _ = qd.shape
    tk = min(2 * tq, s)
    blk = lambda: pl.BlockSpec((None, s, DIFF_V_DIM), lambda i, h: (i, 0, h))
    return pl.pallas_call(
        functools.partial(_diff_prompt_kernel, tq=tq, tk=tk, lam_init=lam_init),
        out_shape=jax.ShapeDtypeStruct((b, s, DIFF_V_W), BF16),
        grid=(b, N_DIFF_HEADS),
        in_specs=[blk(), blk(), blk(),
                  pl.BlockSpec(lam_p.shape, lambda i, h: (0, 0)),
                  pl.BlockSpec((1, DIFF_V_DIM), lambda i, h: (0, 0))],
        out_specs=blk(),
        compiler_params=_params(2),
        name="diff_prompt",
    )(qd, kd, vd, lam_p, g_subln)


def _sb_prompt_kernel(q_ref, k_ref, v_ref, o_ref, *, tq, tk):
    s_len = q_ref.shape[0]
    kb = k_ref[...]
    vb = v_ref[...]
    tri = _suffix_matrix(with_total=False)
    for qi in range(s_len // tq):
        q0 = qi * tq
        kend = q0 + tq
        qq = _split_halves(q_ref[q0:kend, :])
        carry = jnp.zeros((2 * tq, 1), F32)
        acc = jnp.zeros((2 * tq, LANES), F32)
        for c0 in reversed(range(0, kend, tk)):
            c1 = min(c0 + tk, kend)
            z = _dot_nt(qq, kb[c0:c1])
            sp = _softplus2(z)
            diag = c1 > q0
            if diag:
                valid = _block_mask(z.shape, tq, q0, c0, strict=True)
                sp_m = jnp.where(valid, sp, 0.0)
            else:
                sp_m = sp
            n_sub = (c1 - c0) // LANES
            after = [None] * n_sub
            for j in range(n_sub - 1, -1, -1):
                chunk = sp_m[:, j * LANES:(j + 1) * LANES]
                after[j] = _suffix_matmul(chunk, tri) + carry
                carry = carry + jnp.sum(chunk, axis=-1, keepdims=True)
            w = jnp.exp2(z - sp - jnp.concatenate(after, axis=1))
            if diag:
                w = jnp.where(valid, w, 0.0)
            acc = acc + _dot(w.astype(BF16), vb[c0:c1])
        lane = lax.broadcasted_iota(jnp.int32, (tq, LANES), 1)
        o_ref[q0:kend, :] = jnp.where(lane < SB_HEAD_DIM, acc[:tq], acc[tq:]).astype(BF16)


def _sb_prompt(qs, ks, vs, tq):
    b, s, _ = qs.shape
    tk = min(2 * tq, s)
    blk = lambda: pl.BlockSpec((None, s, LANES), lambda i, h: (i, 0, h))
    return pl.pallas_call(
        functools.partial(_sb_prompt_kernel, tq=tq, tk=tk),
        out_shape=jax.ShapeDtypeStruct((b, s, SB_W), BF16),
        grid=(b, SB_W // LANES),
        in_specs=[blk(), blk(), blk()],
        out_specs=blk(),
        compiler_params=_params(2),
        name="sb_prompt",
    )(qs, ks, vs)


def _sample_attn_kernel(pt_ref, qd_ref, kdn_ref, vdn_ref, qs_ref, ksn_ref, vsn_ref, *refs, lam_init, group):
    del pt_ref
    ck_refs, cv_refs = refs[:group], refs[group:2 * group]
    sk_refs, sv_refs = refs[2 * group:3 * group], refs[3 * group:4 * group]
    lam_ref, gsub_ref, od_ref, osb_ref, qbd_s, m_s, l_s, acc_s, c_s, accsb_s = refs[4 * group:]
    p = pl.program_id(1)
    n_steps = pl.num_programs(1)
    t_new = qd_ref.shape[0]
    page = ck_refs[0].shape[0] // N_DIFF_HEADS
    drows = 2 * t_new
    srows = N_SB_HEADS * t_new
    tri = _suffix_matrix(with_total=True)

    def sb_q(h):
        return qs_ref[:, h * SB_HEAD_DIM:(h + 1) * SB_HEAD_DIM]

    def process(s_of, v_of, z_of, pvs_of, n_chunks, new):
        s = jnp.concatenate([s_of(h, qbd_s[h * drows:(h + 1) * drows, :]) for h in range(N_DIFF_HEADS)], axis=0)
        if new:
            t_idx = lax.broadcasted_iota(jnp.int32, s.shape, 0) & (t_new - 1)
            col = lax.broadcasted_iota(jnp.int32, s.shape, 1)
            s = jnp.where(col <= t_idx, s, -jnp.inf)
        m_prev = m_s[...]
        m_new = jnp.maximum(m_prev, jnp.max(s, axis=-1, keepdims=True))
        alpha = jnp.exp2(m_prev - m_new)
        pe = jnp.exp2(s - m_new)
        l_s[...] = alpha * l_s[...] + jnp.sum(pe, axis=-1, keepdims=True)
        m_s[...] = m_new
        pb = pe.astype(BF16)
        pv = jnp.concatenate([_dot(pb[h * drows:(h + 1) * drows], v_of(h)) for h in range(N_DIFF_HEADS)], axis=0)
        acc_s[...] = alpha * acc_s[...] + pv
        z = jnp.concatenate([z_of(h) for h in range(N_SB_HEADS)], axis=0)
        sp = _softplus2(z)
        if new:
            t_idx = lax.broadcasted_iota(jnp.int32, z.shape, 0) & (t_new - 1)
            col = lax.broadcasted_iota(jnp.int32, z.shape, 1)
            valid = col < t_idx
            sp_m = jnp.where(valid, sp, 0.0)
        else:
            sp_m = sp
        stacked = jnp.concatenate([sp_m[:, j * page:(j + 1) * page] for j in range(n_chunks)], axis=0)
        both = _suffix_matmul(stacked, tri)
        suffix, total = both[:, :LANES], both[:, LANES:]
        carry = c_s[...]
        after = []
        for j in range(n_chunks):
            after.append(suffix[j * srows:(j + 1) * srows] + carry)
            carry = carry + total[j * srows:(j + 1) * srows]
        c_s[...] = carry
        w = jnp.exp2(z - sp - jnp.concatenate(after, axis=1))
        if new:
            w = jnp.where(valid, w, 0.0)
        wb = w.astype(BF16)
        pvs = jnp.concatenate([pvs_of(h, wb[h * t_new:(h + 1) * t_new]) for h in range(N_SB_HEADS)], axis=0)
        accsb_s[...] = accsb_s[...] + pvs

    @pl.when(p == 0)
    def _():
        row = lax.broadcasted_iota(jnp.int32, (drows, LANES), 0)
        lane = lax.broadcasted_iota(jnp.int32, (drows, LANES), 1)
        same_comp = (lane >= DIFF_HEAD_DIM) == (row >= t_new)
        for h in range(N_DIFF_HEADS):
            qh = qd_ref[:, h * LANES:(h + 1) * LANES]
            qbd_s[h * drows:(h + 1) * drows, :] = jnp.where(same_comp, jnp.concatenate([qh, qh], axis=0), 0)
        m_s[...] = jnp.full(m_s.shape, -jnp.inf, F32)
        l_s[...] = jnp.zeros(l_s.shape, F32)
        acc_s[...] = jnp.zeros(acc_s.shape, F32)
        c_s[...] = jnp.zeros(c_s.shape, F32)
        accsb_s[...] = jnp.zeros(accsb_s.shape, F32)

        def new_page(x_ref, width):
            def get(h):
                x = x_ref[:, h * width:(h + 1) * width]
                return jnp.concatenate([x, jnp.zeros((page - t_new, width), BF16)], axis=0)
            return get

        kd_new = new_page(kdn_ref, LANES)
        ks_new = new_page(ksn_ref, SB_HEAD_DIM)
        vs_new = new_page(vsn_ref, SB_HEAD_DIM)
        process(lambda h, q: _dot_nt(q, kd_new(h)), new_page(vdn_ref, LANES),
                lambda h: _dot_nt(sb_q(h), ks_new(h)), lambda h, w: _dot(w, vs_new(h)), 1, new=True)

    def head_rows(r, h):
        return r[pl.ds(h, page, stride=N_DIFF_HEADS), :]

    def cached_keys_t(h):
        return jnp.concatenate([head_rows(r, h).T.astype(BF16) for r in ck_refs], axis=1)

    def cached_values(h):
        return jnp.concatenate([head_rows(r, h).astype(BF16) for r in cv_refs], axis=0)

    def cached_cols(x_refs):
        return lambda h: jnp.concatenate([r[h].astype(BF16) for r in x_refs], axis=1)

    sk_of, sv_of = cached_cols(sk_refs), cached_cols(sv_refs)
    process(lambda h, q: _dot(q, cached_keys_t(h)), cached_values,
            lambda h: _dot(sb_q(h), sk_of(h)), lambda h, w: _dot_nt(w, sv_of(h)), group, new=False)

    @pl.when(p == n_steps - 1)
    def _():
        lam = _lam_from_params(lam_ref, lam_init)
        gsub = gsub_ref[...] * (1.0 - lam_init)
        acc = acc_s[...]
        inv_l = 1.0 / l_s[...]
        for h in range(N_DIFF_HEADS):
            a = acc[h * drows:(h + 1) * drows]
            il = inv_l[h * drows:(h + 1) * drows]
            o = a[:t_new] * il[:t_new] - a[t_new:] * (lam * il[t_new:])
            od_ref[:, h * DIFF_V_DIM:(h + 1) * DIFF_V_DIM] = (_rms(o) * gsub).astype(BF16)
        accsb = accsb_s[...]
        for h in range(N_SB_HEADS):
            osb_ref[:, h * SB_HEAD_DIM:(h + 1) * SB_HEAD_DIM] = accsb[h * t_new:(h + 1) * t_new].astype(BF16)


def _sample_attn(page_table, qd, kdn, vdn, qs, ksn, vsn, ck, cv, sk, sv, lam_p, g_subln, lam_init, page_base):
    db, t_new, _ = qd.shape
    n_pages = page_table.shape[1]
    page = sk.shape[3]
    assert t_new == SUBLANES and page == LANES
    group = max(g for g in range(1, MAX_PAGES_PER_STEP + 1) if n_pages % g == 0)

    def new_spec(width):
        return pl.BlockSpec((None, t_new, width), lambda b, p, pt: (b, 0, 0))

    def page_id(b, p, pt, j):
        return pt[b, n_pages - 1 - (p * group + j)] + page_base

    def diff_spec(j):
        return pl.BlockSpec((page * N_DIFF_HEADS, 2 * DIFF_HEAD_DIM), lambda b, p, pt: (page_id(b, p, pt, j), 0))

    def sb_spec(j):
        return pl.BlockSpec((None, N_SB_HEADS, SB_HEAD_DIM, page), lambda b, p, pt: (page_id(b, p, pt, j), 0, 0, 0))

    pages = range(group)
    n_drow = N_DIFF_HEADS * 2 * t_new
    n_srow = N_SB_HEADS * t_new
    grid_spec = pltpu.PrefetchScalarGridSpec(
        num_scalar_prefetch=1,
        grid=(db, n_pages // group),
        in_specs=[new_spec(DIFF_QK_W), new_spec(DIFF_QK_W), new_spec(DIFF_V_W),
                  new_spec(SB_W), new_spec(SB_W), new_spec(SB_W),
                  *[diff_spec(j) for j in pages], *[diff_spec(j) for j in pages],
                  *[sb_spec(j) for j in pages], *[sb_spec(j) for j in pages],
                  pl.BlockSpec(lam_p.shape, lambda b, p, pt: (0, 0)),
                  pl.BlockSpec((1, DIFF_V_DIM), lambda b, p, pt: (0, 0))],
        out_specs=(new_spec(DIFF_V_W), new_spec(SB_W)),
        scratch_shapes=[pltpu.VMEM((n_drow, 2 * DIFF_HEAD_DIM), BF16),
                        pltpu.VMEM((n_drow, 1), F32), pltpu.VMEM((n_drow, 1), F32),
                        pltpu.VMEM((n_drow, DIFF_V_DIM), F32),
                        pltpu.VMEM((n_srow, LANES), F32), pltpu.VMEM((n_srow, SB_HEAD_DIM), F32)],
    )
    return pl.pallas_call(
        functools.partial(_sample_attn_kernel, lam_init=lam_init, group=group),
        out_shape=(jax.ShapeDtypeStruct((db, t_new, DIFF_V_W), BF16), jax.ShapeDtypeStruct((db, t_new, SB_W), BF16)),
        grid_spec=grid_spec,
        compiler_params=_params(2),
        name="sample_attn",
    )(page_table, qd, kdn, vdn, qs, ksn, vsn, *[ck] * group, *[cv] * group, *[sk] * group, *[sv] * group,
      lam_p, g_subln)


def _post_kernel(od_ref, osb_ref, gate_ref, x_ref, gt_ref, shf_ref, scf_ref,
                 wa_ref, wb_ref, wo_ref, gpost_ref, gpre_ref, x1_ref, h2_ref):
    ya = _dot(od_ref[...], wa_ref[...])
    yb = _dot(osb_ref[...], wb_ref[...])
    g = gate_ref[...].astype(F32)
    m = (g[:, :D_MODEL] * ya + g[:, D_MODEL:] * yb).astype(BF16)
    mix = _dot(m, wo_ref[...])
    x1 = x_ref[...] + gt_ref[...] * (_rms(mix) * gpost_ref[...])
    x1_ref[...] = x1
    h2_ref[...] = (_rms(x1) * gpre_ref[...] * (1.0 + scf_ref[...]) + shf_ref[...]).astype(BF16)


def _post(od, osb, gate, x, ada3, wa, wb, wo, g_post, g_pre_ffn, tm, blocks_per_seq):
    r, d = x.shape
    rows_per_mod = ada3.shape[1]

    def mod_spec(col):
        if rows_per_mod == 1:
            return pl.BlockSpec((None, 1, d), lambda i: (i // blocks_per_seq, 0, col))
        return pl.BlockSpec((None, tm, d), lambda i: (0, i, col))

    row_spec = lambda w: pl.BlockSpec((tm, w), lambda i: (i, 0))
    const = lambda a: pl.BlockSpec(a.shape, lambda i: (0, 0), pipeline_mode=pl.Buffered(1))
    return pl.pallas_call(
        _post_kernel,
        out_shape=(jax.ShapeDtypeStruct((r, d), F32), jax.ShapeDtypeStruct((r, d), BF16)),
        grid=(r // tm,),
        in_specs=[row_spec(DIFF_V_W), row_spec(SB_W), row_spec(2 * D_MODEL), row_spec(d),
                  mod_spec(2), mod_spec(3), mod_spec(4),
                  const(wa), const(wb), const(wo), const(g_post), const(g_pre_ffn)],
        out_specs=(row_spec(d), row_spec(d)),
        compiler_params=_params(1),
        name="post_attn",
    )(od, osb, gate, x, ada3, ada3, ada3, wa, wb, wo, g_post, g_pre_ffn)


def _gelu_tanh(x):
    return 0.5 * x * (1.0 + jnp.tanh(math.sqrt(2.0 / math.pi) * (x + 0.044715 * (x * x * x))))


def _ffn_tail(u, um1, um2, cw_ref, cb_ref, wd_ref, x1_ref, gt_ref, gpost_ref, y_ref):
    conv = cb_ref[...] + um2 * cw_ref[0:1, :] + um1 * cw_ref[1:2, :] + u * cw_ref[2:3, :]
    y = (_gelu_tanh(conv[:, :D_FF]) * conv[:, D_FF:]).astype(BF16)
    f = _dot(y, wd_ref[...])
    y_ref[...] = x1_ref[...] + gt_ref[...] * (_rms(f) * gpost_ref[...])


def _ffn_prompt_kernel(h2_ref, x1_ref, gt_ref, wu_ref, cw_ref, cb_ref, wd_ref, gpost_ref,
                       y_ref, st_ref, ext_s, *, blocks_per_seq):
    tm = h2_ref.shape[0]
    i = pl.program_id(0)

    @pl.when(i % blocks_per_seq == 0)
    def _():
        ext_s[0:SUBLANES, :] = jnp.zeros((SUBLANES, ext_s.shape[1]), F32)

    u = _dot(h2_ref[...], wu_ref[...])
    ext_s[SUBLANES:, :] = u
    um1 = ext_s[SUBLANES - 1:SUBLANES - 1 + tm, :]
    um2 = ext_s[SUBLANES - 2:SUBLANES - 2 + tm, :]
    _ffn_tail(u, um1, um2, cw_ref, cb_ref, wd_ref, x1_ref, gt_ref, gpost_ref, y_ref)
    st_ref[...] = ext_s[tm + SUBLANES - (CONV_WIDTH - 1):, :]
    ext_s[0:SUBLANES, :] = ext_s[tm:tm + SUBLANES, :]


def _ffn_prompt(h2, x1, ada3, wu, cw, cb, wd, g_post, tm, blocks_per_seq):
    r, d = x1.shape
    n_seq = r // (tm * blocks_per_seq)
    const = lambda a: pl.BlockSpec(a.shape, lambda i: (0, 0), pipeline_mode=pl.Buffered(1))
    row_spec = lambda w: pl.BlockSpec((tm, w), lambda i: (i, 0))
    return pl.pallas_call(
        functools.partial(_ffn_prompt_kernel, blocks_per_seq=blocks_per_seq),
        out_shape=(jax.ShapeDtypeStruct((r, d), F32),
                   jax.ShapeDtypeStruct((n_seq, CONV_WIDTH - 1, 2 * D_FF), F32)),
        grid=(r // tm,),
        in_specs=[row_spec(d), row_spec(d),
                  pl.BlockSpec((None, 1, d), lambda i: (i // blocks_per_seq, 0, 5)),
                  const(wu), const(cw), const(cb), const(wd), const(g_post)],
        out_specs=(row_spec(d),
                   pl.BlockSpec((None, CONV_WIDTH - 1, 2 * D_FF), lambda i: (i // blocks_per_seq, 0, 0))),
        scratch_shapes=[pltpu.VMEM((tm + SUBLANES, 2 * D_FF), F32)],
        compiler_params=_params(1),
        name="ffn_prompt",
    )(h2, x1, ada3, wu, cw, cb, wd, g_post)


def _ffn_sample_kernel(h2_ref, x1_ref, gt_ref, st0_ref, st1_ref, wu_ref, cw_ref, cb_ref, wd_ref, gpost_ref,
                       y_ref, u_ref, *, t_new):
    u = _dot(h2_ref[...], wu_ref[...])
    u_ref[...] = u
    t_idx = lax.broadcasted_iota(jnp.int32, u.shape, 0) & (t_new - 1)
    st0 = st0_ref[...]
    st1 = st1_ref[...]
    um1 = jnp.where(t_idx >= 1, pltpu.roll(u, 1, 0), st1)
    um2 = jnp.where(t_idx >= 2, pltpu.roll(u, 2, 0), jnp.where(t_idx == 0, st0, st1))
    _ffn_tail(u, um1, um2, cw_ref, cb_ref, wd_ref, x1_ref, gt_ref, gpost_ref, y_ref)


def _ffn_sample(h2, x1, gt_rows, st0_rows, st1_rows, wu, cw, cb, wd, g_post, t_new):
    r, d = x1.shape
    return pl.pallas_call(
        functools.partial(_ffn_sample_kernel, t_new=t_new),
        out_shape=(jax.ShapeDtypeStruct((r, d), F32), jax.ShapeDtypeStruct((r, 2 * D_FF), F32)),
        compiler_params=pltpu.CompilerParams(vmem_limit_bytes=VMEM_LIMIT_BYTES),
        name="ffn_sample",
    )(h2, x1, gt_rows, st0_rows, st1_rows, wu, cw, cb, wd, g_post)


def _pick_tile(n, target):
    t = min(n, target)
    while n % t:
        t //= 2
    return t


def _stack_layers(vals):
    return vals[0][None] if len(vals) == 1 else jnp.stack(vals, axis=0)


def kernel(x_prompt, x_sample, c_prompt, c_sample, cache_diff_k, cache_diff_v, cache_sb_k, cache_sb_v, state_conv, page_table, w_ada, b_ada, g_pre_mix, g_post_mix, g_pre_ffn, g_post_ffn, w_in, lam_q1, lam_k1, lam_q2, lam_k2, g_subln, w_br_a, w_br_b, w_out, w_up, conv_w, conv_b, w_down):
    b, s, d = x_prompt.shape
    db, t_new, _ = x_sample.shape
    depth = w_in.shape[0]
    n_phys, page = cache_diff_k.shape[1], cache_diff_k.shape[2]
    n_past = page_table.shape[1] * page
    assert d == D_MODEL and t_new == SUBLANES and page == LANES

    tm = _pick_tile(s, 256)
    tq = _pick_tile(s, 256)
    bps = s // tm
    r_s = db * t_new

    tab_p = _rope_tables(jnp.arange(s, dtype=jnp.int32))
    tab_s = tuple(jnp.tile(t, (db, 1)) for t in _rope_tables(n_past + jnp.arange(t_new, dtype=jnp.int32)))

    ck = cache_diff_k.reshape(-1, 2 * DIFF_HEAD_DIM)
    cv = cache_diff_v.reshape(-1, DIFF_V_DIM)
    sb_view = lambda c: jnp.transpose(c, (0, 1, 3, 4, 2)).reshape(depth * n_phys, N_SB_HEADS, SB_HEAD_DIM, page)
    sk = sb_view(cache_sb_k)
    sv = sb_view(cache_sb_v)

    hp = x_prompt.reshape(b * s, d)
    hs = x_sample.reshape(r_s, d)
    c_all = jnp.concatenate([c_prompt, c_sample], axis=0)
    new_p = [[] for _ in range(5)]
    new_s = [[] for _ in range(5)]
    for l in range(depth):
        lam_init = 0.8 - 0.6 * math.exp(-0.3 * l)
        bf = lambda w: w[l].astype(BF16)
        row = lambda v: v[l].reshape(1, -1)
        lam_p = jnp.stack([lam_q1[l], lam_k1[l], lam_q2[l], lam_k2[l]], axis=0)
        ada = _ada(c_all, bf(w_ada), row(b_ada))
        ada_p = ada[:b].reshape(b, 1, 6 * d)
        ada_s = jnp.repeat(ada[b:], t_new, axis=0).reshape(1, r_s, 6 * d)
        w_in_l, wa, wb, wo, wu, wd = bf(w_in), bf(w_br_a), bf(w_br_b), bf(w_out), bf(w_up), bf(w_down)
        cw, cb = conv_w[l], row(conv_b)
        gsub = row(g_subln)

        qd, kdb, vdb, qs, ksb, vsb, gate, kd4, vd4, ks4, vs4 = _in_proj(
            hp, ada_p, row(g_pre_mix), w_in_l, tab_p, tm, bps, True)
        sb_cache = lambda a: jnp.transpose(a.reshape(b, N_SB_HEADS, SB_HEAD_DIM, s), (0, 3, 1, 2))
        r3 = lambda a: a.reshape(b, s, -1)
        od = _diff_prompt(r3(qd), r3(kdb), r3(vdb), lam_p, gsub, lam_init, tq)
        osb = _sb_prompt(r3(qs), r3(ksb), r3(vsb), tq)
        x1, h2 = _post(od.reshape(b * s, -1), osb.reshape(b * s, -1), gate, hp, ada_p, wa, wb, wo,
                       row(g_post_mix), row(g_pre_ffn), tm, bps)
        hp, conv_p = _ffn_prompt(h2, x1, ada_p, wu, cw, cb, wd, row(g_post_ffn), tm, bps)
        for lst, val in zip(new_p, (kd4.reshape(b, s, N_DIFF_HEADS, 2 * DIFF_HEAD_DIM),
                                    vd4.reshape(b, s, N_DIFF_HEADS, DIFF_V_DIM),
                                    sb_cache(ks4), sb_cache(vs4), conv_p)):
            lst.append(val)

        qd, kdb, vdb, qs, ksb, vsb, gate, kd4, vd4, ks4, vs4 = _in_proj(
            hs, ada_s, row(g_pre_mix), w_in_l, tab_s, r_s, 1, False)
        r3 = lambda a: a.reshape(db, t_new, -1)
        od, osb = _sample_attn(page_table, r3(qd), r3(kdb), r3(vdb), r3(qs), r3(ksb), r3(vsb),
                               ck, cv, sk, sv, lam_p, gsub, lam_init, l * n_phys)
        x1, h2 = _post(od.reshape(r_s, -1), osb.reshape(r_s, -1), gate, hs, ada_s, wa, wb, wo,
                       row(g_post_mix), row(g_pre_ffn), r_s, 1)
        st = state_conv[l]
        hs, u_s = _ffn_sample(h2, x1, ada_s[0, :, 5 * d:], jnp.repeat(st[:, 0], t_new, axis=0),
                              jnp.repeat(st[:, 1], t_new, axis=0), wu, cw, cb, wd, row(g_post_ffn), t_new)
        for lst, val in zip(new_s, (kd4.reshape(db, t_new, N_DIFF_HEADS, 2 * DIFF_HEAD_DIM),
                                    vd4.reshape(db, t_new, N_DIFF_HEADS, DIFF_V_DIM),
                                    ks4.reshape(db, t_new, N_SB_HEADS, SB_HEAD_DIM),
                                    vs4.reshape(db, t_new, N_SB_HEADS, SB_HEAD_DIM),
                                    u_s.reshape(db, t_new, 2 * D_FF)[:, t_new - (CONV_WIDTH - 1):])):
            lst.append(val)

    outs_p = [_stack_layers(a) for a in new_p]
    outs_s = [_stack_layers(a) for a in new_s]
    return (hp.reshape(b, s, d), hs.reshape(db, t_new, d), *outs_p, *outs_s)
```

```python
import functools
import math

import jax
import jax.numpy as jnp
from jax import lax
from jax.experimental import pallas as pl
from jax.experimental.pallas import tpu as pltpu

D_MODEL = 1024
N_DIFF_HEADS = 8
DIFF_HEAD_DIM = 64
DIFF_V_DIM = 2 * DIFF_HEAD_DIM
N_SB_HEADS = 8
SB_HEAD_DIM = 64
ROPE_DIM = DIFF_HEAD_DIM // 4
ROPE_HALF = ROPE_DIM // 2
ROPE_THETA = 500000.0
D_FF = 2816
CONV_WIDTH = 3
EPS = 1e-6
DIFF_QK_W = N_DIFF_HEADS * 2 * DIFF_HEAD_DIM
DIFF_V_W = N_DIFF_HEADS * DIFF_V_DIM
SB_W = N_SB_HEADS * SB_HEAD_DIM
IN_PROJ_W = 2 * DIFF_QK_W + DIFF_V_W + 3 * SB_W + 2 * D_MODEL
LOG2E = math.log2(math.e)
Q_SCALE2 = DIFF_HEAD_DIM ** -0.5 * LOG2E
SOFTPLUS2_CLAMP = 64.0

LANES = 128
SUBLANES = 8
VMEM_LIMIT_BYTES = 56 * 1024 * 1024
MAX_PAGES_PER_STEP = 8

F32 = jnp.float32
BF16 = jnp.bfloat16


def _params(n_grid_dims, vmem=VMEM_LIMIT_BYTES):
    return pltpu.CompilerParams(dimension_semantics=("arbitrary",) * n_grid_dims, vmem_limit_bytes=vmem)


def _dot(a, b):
    return jnp.dot(a, b, preferred_element_type=F32)


def _dot_nt(a, b):
    return lax.dot_general(a, b, (((1,), (1,)), ((), ())), preferred_element_type=F32)


def _sigmoid(x):
    return 1.0 / (1.0 + jnp.exp(-x))


def _rms(x):
    return x * lax.rsqrt(jnp.mean(x * x, axis=-1, keepdims=True) + EPS)


def _softplus2(z2):
    return jnp.maximum(jnp.log(1.0 + jnp.exp2(jnp.minimum(z2, SOFTPLUS2_CLAMP))) * LOG2E, z2)


def _lam_from_params(lam_ref, lam_init):
    lp = lam_ref[...]
    s1 = jnp.sum(lp[0:1] * lp[1:2], axis=-1, keepdims=True)
    s2 = jnp.sum(lp[2:3] * lp[3:4], axis=-1, keepdims=True)
    return jnp.exp(s1) - jnp.exp(s2) + lam_init


def _suffix_matrix(with_total):
    width = 2 * LANES if with_total else LANES
    j = lax.broadcasted_iota(jnp.int32, (2 * LANES, width), 0) & (LANES - 1)
    s = lax.broadcasted_iota(jnp.int32, (2 * LANES, width), 1)
    return jnp.where((j > s) | (s >= LANES), 1.0, 0.0).astype(BF16)


def _suffix_matmul(x, tri):
    hi = x.astype(BF16)
    lo = (x - hi.astype(F32)).astype(BF16)
    return _dot(jnp.concatenate([hi, lo], axis=1), tri)


def _ada_kernel(c_ref, w_ref, b_ref, o_ref):
    c = c_ref[...]
    a = (c * _sigmoid(c)).astype(BF16)
    o_ref[...] = _dot(a, w_ref[...]) + b_ref[...]


def _ada(c, w_bf16, b):
    n, d = c.shape
    n_out = w_bf16.shape[1]
    return pl.pallas_call(
        _ada_kernel,
        out_shape=jax.ShapeDtypeStruct((n, n_out), F32),
        grid=(n_out // d,),
        in_specs=[
            pl.BlockSpec((n, d), lambda j: (0, 0)),
            pl.BlockSpec((d, d), lambda j: (0, j)),
            pl.BlockSpec((1, d), lambda j: (0, j)),
        ],
        out_specs=pl.BlockSpec((n, d), lambda j: (0, j)),
        compiler_params=_params(1),
        name="ada",
    )(c, w_bf16, b)


def _rope_table_kernel(pos_ref, invf_ref, c_ref, s1_ref, s2_ref):
    ang = pos_ref[...] * invf_ref[...]
    r = lax.broadcasted_iota(jnp.int32, ang.shape, 1) & (DIFF_HEAD_DIM - 1)
    cos = jnp.cos(ang)
    sin = jnp.sin(ang)
    c_ref[...] = jnp.where(r < ROPE_DIM, cos, 1.0)
    s1_ref[...] = jnp.where((r >= ROPE_HALF) & (r < ROPE_DIM), sin, 0.0)
    s2_ref[...] = jnp.where(r < ROPE_HALF, -sin, 0.0)


def _rope_tables(pos):
    p = pos.shape[0]
    inv_freq = jnp.power(jnp.float32(ROPE_THETA), -jnp.arange(ROPE_HALF, dtype=F32) * (2.0 / ROPE_DIM))
    r = jnp.arange(LANES) % DIFF_HEAD_DIM
    invf_lane = jnp.where(r < ROPE_DIM, inv_freq[r % ROPE_HALF], 0.0).reshape(1, LANES).astype(F32)
    shp = jax.ShapeDtypeStruct((p, LANES), F32)
    return pl.pallas_call(
        _rope_table_kernel,
        out_shape=(shp, shp, shp),
        name="rope_tables",
    )(pos.astype(F32).reshape(p, 1), invf_lane)


def _inproj_kernel(x_ref, sh_ref, sc_ref, g_ref, w_ref, c_ref, s1_ref, s2_ref,
                   qd_ref, kdb_ref, vdb_ref, qs_ref, ksb_ref, vsb_ref, gate_ref,
                   kd4_ref, vd4_ref, ks4_ref, vs4_ref, *, sb_token_minor):
    x = x_ref[...]
    tm = x.shape[0]
    h = (_rms(x) * g_ref[...] * (1.0 + sc_ref[...]) + sh_ref[...]).astype(BF16)
    cc, s1, s2 = c_ref[...], s1_ref[...], s2_ref[...]

    def rope(u):
        return u * cc + pltpu.roll(u, ROPE_HALF, 1) * s1 + pltpu.roll(u, LANES - ROPE_HALF, 1) * s2

    o_k = DIFF_QK_W
    o_v = o_k + DIFF_QK_W
    o_sq = o_v + DIFF_V_W
    o_sk = o_sq + SB_W
    o_sv = o_sk + SB_W
    o_g = o_sv + SB_W
    for j2 in range(0, N_DIFF_HEADS, 2):
        a2 = j2 * LANES
        uq = _dot(h, w_ref[:, a2:a2 + 2 * LANES])
        uk = _dot(h, w_ref[:, o_k + a2:o_k + a2 + 2 * LANES])
        uv = _dot(h, w_ref[:, o_v + a2:o_v + a2 + 2 * LANES])
        for j in (j2, j2 + 1):
            a = j * LANES
            lo = a - a2
            qd_ref[:, a:a + LANES] = (rope(uq[:, lo:lo + LANES]) * Q_SCALE2).astype(BF16)
            k = rope(uk[:, lo:lo + LANES])
            kdb_ref[:, a:a + LANES] = k.astype(BF16)
            kd4_ref[pl.ds(j, tm, stride=N_DIFF_HEADS), :] = k
            v = uv[:, lo:lo + LANES]
            vdb_ref[:, a:a + LANES] = v.astype(BF16)
            vd4_ref[pl.ds(j, tm, stride=N_DIFF_HEADS), :] = v
    qs_ref[...] = (_dot(h, w_ref[:, o_sq:o_sk]) * Q_SCALE2).astype(BF16)
    ks = _dot(h, w_ref[:, o_sk:o_sv])
    vs = _dot(h, w_ref[:, o_sv:o_g])
    ksb_ref[...] = ks.astype(BF16)
    vsb_ref[...] = vs.astype(BF16)
    if sb_token_minor:
        ks4_ref[...] = ks.T
        vs4_ref[...] = vs.T
    else:
        for j in range(N_SB_HEADS):
            a = j * SB_HEAD_DIM
            ks4_ref[pl.ds(j, tm, stride=N_SB_HEADS), :] = ks[:, a:a + SB_HEAD_DIM]
            vs4_ref[pl.ds(j, tm, stride=N_SB_HEADS), :] = vs[:, a:a + SB_HEAD_DIM]
    gate_ref[...] = _sigmoid(_dot(h, w_ref[:, o_g:])).astype(BF16)


def _in_proj(x, ada3, g_pre, w_in_bf16, tables, tm, blocks_per_seq, sb_token_minor):
    r, d = x.shape
    rows_per_mod = ada3.shape[1]
    n_tab_blocks = tables[0].shape[0] // tm

    def mod_spec(col):
        if rows_per_mod == 1:
            return pl.BlockSpec((None, 1, d), lambda i: (i // blocks_per_seq, 0, col))
        return pl.BlockSpec((None, tm, d), lambda i: (0, i, col))

    tab_spec = pl.BlockSpec((tm, LANES), lambda i: (i % n_tab_blocks, 0))

    def out(width, dtype, heads=1):
        return (jax.ShapeDtypeStruct((r * heads, width), dtype),
                pl.BlockSpec((tm * heads, width), lambda i: (i, 0)))

    if sb_token_minor:
        seq = tm * blocks_per_seq
        sb_out = (jax.ShapeDtypeStruct((r // seq * SB_W, seq), F32),
                  pl.BlockSpec((SB_W, tm), lambda i: (i // blocks_per_seq, i % blocks_per_seq)))
    else:
        sb_out = out(SB_HEAD_DIM, F32, N_SB_HEADS)
    outs = [out(DIFF_QK_W, BF16), out(DIFF_QK_W, BF16), out(DIFF_V_W, BF16),
            out(SB_W, BF16), out(SB_W, BF16), out(SB_W, BF16), out(2 * D_MODEL, BF16),
            out(2 * DIFF_HEAD_DIM, F32, N_DIFF_HEADS), out(DIFF_V_DIM, F32, N_DIFF_HEADS),
            sb_out, sb_out]
    return pl.pallas_call(
        functools.partial(_inproj_kernel, sb_token_minor=sb_token_minor),
        out_shape=tuple(o[0] for o in outs),
        grid=(r // tm,),
        in_specs=[
            pl.BlockSpec((tm, d), lambda i: (i, 0)),
            mod_spec(0), mod_spec(1),
            pl.BlockSpec((1, d), lambda i: (0, 0)),
            pl.BlockSpec((d, IN_PROJ_W), lambda i: (0, 0), pipeline_mode=pl.Buffered(1)),
            tab_spec, tab_spec, tab_spec,
        ],
        out_specs=tuple(o[1] for o in outs),
        compiler_params=_params(1),
        name="in_proj",
    )(x, ada3, ada3, g_pre, w_in_bf16, *tables)


def _split_halves(q):
    lane = lax.broadcasted_iota(jnp.int32, q.shape, 1)
    return jnp.concatenate([jnp.where(lane < DIFF_HEAD_DIM, q, 0), jnp.where(lane >= DIFF_HEAD_DIM, q, 0)], axis=0)


def _block_mask(shape, tq, q0, c0, strict):
    row = (lax.broadcasted_iota(jnp.int32, shape, 0) & (tq - 1)) + q0
    col = lax.broadcasted_iota(jnp.int32, shape, 1) + c0
    return (col < row) if strict else (col <= row)


def _diff_prompt_kernel(q_ref, k_ref, v_ref, lam_ref, gsub_ref, o_ref, *, tq, tk, lam_init):
    s_len = q_ref.shape[0]
    kb = k_ref[...]
    vaug = jnp.concatenate([v_ref[...], jnp.ones((s_len, LANES), BF16)], axis=1)
    lam = _lam_from_params(lam_ref, lam_init)
    gsub = gsub_ref[...] * (1.0 - lam_init)
    for qi in range(s_len // tq):
        q0 = qi * tq
        kend = q0 + tq
        qq = _split_halves(q_ref[q0:kend, :])
        m = jnp.full((2 * tq, 1), -jnp.inf, F32)
        acc = jnp.zeros((2 * tq, 2 * LANES), F32)
        for c0 in range(0, kend, tk):
            c1 = min(c0 + tk, kend)
            s = _dot_nt(qq, kb[c0:c1])
            if c1 > q0:
                s = jnp.where(_block_mask(s.shape, tq, q0, c0, strict=False), s, -jnp.inf)
            m_new = jnp.maximum(m, jnp.max(s, axis=-1, keepdims=True))
            p = jnp.exp2(s - m_new).astype(BF16)
            acc = jnp.exp2(m - m_new) * acc + _dot(p, vaug[c0:c1])
            m = m_new
        inv_l = 1.0 / acc[:, LANES:LANES + 1]
        o = acc[:tq, :LANES] * inv_l[:tq] - acc[tq:, :LANES] * (lam * inv_l[tq:])
        o_ref[q0:kend, :] = (_rms(o) * gsub).astype(BF16)


def _diff_prompt(qd, kd, vd, lam_p, g_subln, lam_init, tq):
    b, s, _ = qd.shape
    tk = min(2 * tq, s)
    blk = lambda: pl.BlockSpec((None, s, DIFF_V_DIM), lambda i, h: (i, 0, h))
    return pl.pallas_call(
        functools.partial(_diff_prompt_kernel, tq=tq, tk=tk, lam_init=lam_init),
        out_shape=jax.ShapeDtypeStruct((b, s, DIFF_V_W), BF16),
        grid=(b, N_DIFF_HEADS),
        in_specs=[blk(), blk(), blk(),
                  pl.BlockSpec(lam_p.shape, lambda i, h: (0, 0)),
                  pl.BlockSpec((1, DIFF_V_DIM), lambda i, h: (0, 0))],
        out_specs=blk(),
        compiler_params=_params(2),
        name="diff_prompt",
    )(qd, kd, vd, lam_p, g_subln)


def _sb_prompt_kernel(q_ref, k_ref, v_ref, o_ref, *, tq, tk):
    s_len = q_ref.shape[0]
    kb = k_ref[...]
    vb = v_ref[...]
    tri = _suffix_matrix(with_total=False)
    for qi in range(s_len // tq):
        q0 = qi * tq
        kend = q0 + tq
        qq = _split_halves(q_ref[q0:kend, :])
        carry = jnp.zeros((2 * tq, 1), F32)
        acc = jnp.zeros((2 * tq, LANES), F32)
        for c0 in reversed(range(0, kend, tk)):
            c1 = min(c0 + tk, kend)
            z = _dot_nt(qq, kb[c0:c1])
            sp = _softplus2(z)
            diag = c1 > q0
            if diag:
                valid = _block_mask(z.shape, tq, q0, c0, strict=True)
                sp_m = jnp.where(valid, sp, 0.0)
            else:
                sp_m = sp
            n_sub = (c1 - c0) // LANES
            after = [None] * n_sub
            for j in range(n_sub - 1, -1, -1):
                chunk = sp_m[:, j * LANES:(j + 1) * LANES]
                after[j] = _suffix_matmul(chunk, tri) + carry
                carry = carry + jnp.sum(chunk, axis=-1, keepdims=True)
            w = jnp.exp2(z - sp - jnp.concatenate(after, axis=1))
            if diag:
                w = jnp.where(valid, w, 0.0)
            acc = acc + _dot(w.astype(BF16), vb[c0:c1])
        lane = lax.broadcasted_iota(jnp.int32, (tq, LANES), 1)
        o_ref[q0:kend, :] = jnp.where(lane < SB_HEAD_DIM, acc[:tq], acc[tq:]).astype(BF16)


def _sb_prompt(qs, ks, vs, tq):
    b, s, _ = qs.shape
    tk = min(2 * tq, s)
    blk = lambda: pl.BlockSpec((None, s, LANES), lambda i, h: (i, 0, h))
    return pl.pallas_call(
        functools.partial(_sb_prompt_kernel, tq=tq, tk=tk),
        out_shape=jax.ShapeDtypeStruct((b, s, SB_W), BF16),
        grid=(b, SB_W // LANES),
        in_specs=[blk(), blk(), blk()],
        out_specs=blk(),
        compiler_params=_params(2),
        name="sb_prompt",
    )(qs, ks, vs)


def _sample_attn_kernel(pt_ref, qd_ref, kdn_ref, vdn_ref, qs_ref, ksn_ref, vsn_ref, *refs, lam_init, group):
    del pt_ref
    ck_refs, cv_refs = refs[:group], refs[group:2 * group]
    sk_refs, sv_refs = refs[2 * group:3 * group], refs[3 * group:4 * group]
    lam_ref, gsub_ref, od_ref, osb_ref, qbd_s, m_s, l_s, acc_s, c_s, accsb_s = refs[4 * group:]
    p = pl.program_id(1)
    n_steps = pl.num_programs(1)
    t_new = qd_ref.shape[0]
    page = ck_refs[0].shape[0] // N_DIFF_HEADS
    drows = 2 * t_new
    srows = N_SB_HEADS * t_new
    tri = _suffix_matrix(with_total=True)

    def sb_q(h):
        return qs_ref[:, h * SB_HEAD_DIM:(h + 1) * SB_HEAD_DIM]

    def process(s_of, v_of, z_of, pvs_of, n_chunks, new):
        s = jnp.concatenate([s_of(h, qbd_s[h * drows:(h + 1) * drows, :]) for h in range(N_DIFF_HEADS)], axis=0)
        if new:
            t_idx = lax.broadcasted_iota(jnp.int32, s.shape, 0) & (t_new - 1)
            col = lax.broadcasted_iota(jnp.int32, s.shape, 1)
            s = jnp.where(col <= t_idx, s, -jnp.inf)
        m_prev = m_s[...]
        m_new = jnp.maximum(m_prev, jnp.max(s, axis=-1, keepdims=True))
        alpha = jnp.exp2(m_prev - m_new)
        pe = jnp.exp2(s - m_new)
        l_s[...] = alpha * l_s[...] + jnp.sum(pe, axis=-1, keepdims=True)
        m_s[...] = m_new
        pb = pe.astype(BF16)
        pv = jnp.concatenate([_dot(pb[h * drows:(h + 1) * drows], v_of(h)) for h in range(N_DIFF_HEADS)], axis=0)
        acc_s[...] = alpha * acc_s[...] + pv
        z = jnp.concatenate([z_of(h) for h in range(N_SB_HEADS)], axis=0)
        sp = _softplus2(z)
        if new:
            t_idx = lax.broadcasted_iota(jnp.int32, z.shape, 0) & (t_new - 1)
            col = lax.broadcasted_iota(jnp.int32, z.shape, 1)
            valid = col < t_idx
            sp_m = jnp.where(valid, sp, 0.0)
        else:
            sp_m = sp
        stacked = jnp.concatenate([sp_m[:, j * page:(j + 1) * page] for j in range(n_chunks)], axis=0)
        both = _suffix_matmul(stacked, tri)
        suffix, total = both[:, :LANES], both[:, LANES:]
        carry = c_s[...]
        after = []
        for j in range(n_chunks):
            after.append(suffix[j * srows:(j + 1) * srows] + carry)
            carry = carry + total[j * srows:(j + 1) * srows]
        c_s[...] = carry
        w = jnp.exp2(z - sp - jnp.concatenate(after, axis=1))
        if new:
            w = jnp.where(valid, w, 0.0)
        wb = w.astype(BF16)
        pvs = jnp.concatenate([pvs_of(h, wb[h * t_new:(h + 1) * t_new]) for h in range(N_SB_HEADS)], axis=0)
        accsb_s[...] = accsb_s[...] + pvs

    @pl.when(p == 0)
    def _():
        row = lax.broadcasted_iota(jnp.int32, (drows, LANES), 0)
        lane = lax.broadcasted_iota(jnp.int32, (drows, LANES), 1)
        same_comp = (lane >= DIFF_HEAD_DIM) == (row >= t_new)
        for h in range(N_DIFF_HEADS):
            qh = qd_ref[:, h * LANES:(h + 1) * LANES]
            qbd_s[h * drows:(h + 1) * drows, :] = jnp.where(same_comp, jnp.concatenate([qh, qh], axis=0), 0)
        m_s[...] = jnp.full(m_s.shape, -jnp.inf, F32)
        l_s[...] = jnp.zeros(l_s.shape, F32)
        acc_s[...] = jnp.zeros(acc_s.shape, F32)
        c_s[...] = jnp.zeros(c_s.shape, F32)
        accsb_s[...] = jnp.zeros(accsb_s.shape, F32)

        def new_page(x_ref, width):
            def get(h):
                x = x_ref[:, h * width:(h + 1) * width]
                return jnp.concatenate([x, jnp.zeros((page - t_new, width), BF16)], axis=0)
            return get

        kd_new = new_page(kdn_ref, LANES)
        ks_new = new_page(ksn_ref, SB_HEAD_DIM)
        vs_new = new_page(vsn_ref, SB_HEAD_DIM)
        process(lambda h, q: _dot_nt(q, kd_new(h)), new_page(vdn_ref, LANES),
                lambda h: _dot_nt(sb_q(h), ks_new(h)), lambda h, w: _dot(w, vs_new(h)), 1, new=True)

    def head_rows(r, h):
        return r[pl.ds(h, page, stride=N_DIFF_HEADS), :]

    def cached_keys_t(h):
        return jnp.concatenate([head_rows(r, h).T.astype(BF16) for r in ck_refs], axis=1)

    def cached_values(h):
        return jnp.concatenate([head_rows(r, h).astype(BF16) for r in cv_refs], axis=0)

    def cached_cols(x_refs):
        return lambda h: jnp.concatenate([r[h].astype(BF16) for r in x_refs], axis=1)

    sk_of, sv_of = cached_cols(sk_refs), cached_cols(sv_refs)
    process(lambda h, q: _dot(q, cached_keys_t(h)), cached_values,
            lambda h: _dot(sb_q(h), sk_of(h)), lambda h, w: _dot_nt(w, sv_of(h)), group, new=False)

    @pl.when(p == n_steps - 1)
    def _():
        lam = _lam_from_params(lam_ref, lam_init)
        gsub = gsub_ref[...] * (1.0 - lam_init)
        acc = acc_s[...]
        inv_l = 1.0 / l_s[...]
        for h in range(N_DIFF_HEADS):
            a = acc[h * drows:(h + 1) * drows]
            il = inv_l[h * drows:(h + 1) * drows]
            o = a[:t_new] * il[:t_new] - a[t_new:] * (lam * il[t_new:])
            od_ref[:, h * DIFF_V_DIM:(h + 1) * DIFF_V_DIM] = (_rms(o) * gsub).astype(BF16)
        accsb = accsb_s[...]
        for h in range(N_SB_HEADS):
            osb_ref[:, h * SB_HEAD_DIM:(h + 1) * SB_HEAD_DIM] = accsb[h * t_new:(h + 1) * t_new].astype(BF16)


def _sample_attn(page_table, qd, kdn, vdn, qs, ksn, vsn, ck, cv, sk, sv, lam_p, g_subln, lam_init, page_base):
    db, t_new, _ = qd.shape
    n_pages = page_table.shape[1]
    page = sk.shape[3]
    assert t_new == SUBLANES and page == LANES
    group = max(g for g in range(1, MAX_PAGES_PER_STEP + 1) if n_pages % g == 0)

    def new_spec(width):
        return pl.BlockSpec((None, t_new, width), lambda b, p, pt: (b, 0, 0))

    def page_id(b, p, pt, j):
        return pt[b, n_pages - 1 - (p * group + j)] + page_base

    def diff_spec(j):
        return pl.BlockSpec((page * N_DIFF_HEADS, 2 * DIFF_HEAD_DIM), lambda b, p, pt: (page_id(b, p, pt, j), 0))

    def sb_spec(j):
        return pl.BlockSpec((None, N_SB_HEADS, SB_HEAD_DIM, page), lambda b, p, pt: (page_id(b, p, pt, j), 0, 0, 0))

    pages = range(group)
    n_drow = N_DIFF_HEADS * 2 * t_new
    n_srow = N_SB_HEADS * t_new
    grid_spec = pltpu.PrefetchScalarGridSpec(
        num_scalar_prefetch=1,
        grid=(db, n_pages // group),
        in_specs=[new_spec(DIFF_QK_W), new_spec(DIFF_QK_W), new_spec(DIFF_V_W),
                  new_spec(SB_W), new_spec(SB_W), new_spec(SB_W),
                  *[diff_spec(j) for j in pages], *[diff_spec(j) for j in pages],
                  *[sb_spec(j) for j in pages], *[sb_spec(j) for j in pages],
                  pl.BlockSpec(lam_p.shape, lambda b, p, pt: (0, 0)),
                  pl.BlockSpec((1, DIFF_V_DIM), lambda b, p, pt: (0, 0))],
        out_specs=(new_spec(DIFF_V_W), new_spec(SB_W)),
        scratch_shapes=[pltpu.VMEM((n_drow, 2 * DIFF_HEAD_DIM), BF16),
                        pltpu.VMEM((n_drow, 1), F32), pltpu.VMEM((n_drow, 1), F32),
                        pltpu.VMEM((n_drow, DIFF_V_DIM), F32),
                        pltpu.VMEM((n_srow, LANES), F32), pltpu.VMEM((n_srow, SB_HEAD_DIM), F32)],
    )
    return pl.pallas_call(
        functools.partial(_sample_attn_kernel, lam_init=lam_init, group=group),
        out_shape=(jax.ShapeDtypeStruct((db, t_new, DIFF_V_W), BF16), jax.ShapeDtypeStruct((db, t_new, SB_W), BF16)),
        grid_spec=grid_spec,
        compiler_params=_params(2),
        name="sample_attn",
    )(page_table, qd, kdn, vdn, qs, ksn, vsn, *[ck] * group, *[cv] * group, *[sk] * group, *[sv] * group,
      lam_p, g_subln)


def _post_kernel(od_ref, osb_ref, gate_ref, x_ref, gt_ref, shf_ref, scf_ref,
                 wa_ref, wb_ref, wo_ref, gpost_ref, gpre_ref, x1_ref, h2_ref):
    ya = _dot(od_ref[...], wa_ref[...])
    yb = _dot(osb_ref[...], wb_ref[...])
    g = gate_ref[...].astype(F32)
    m = (g[:, :D_MODEL] * ya + g[:, D_MODEL:] * yb).astype(BF16)
    mix = _dot(m, wo_ref[...])
    x1 = x_ref[...] + gt_ref[...] * (_rms(mix) * gpost_ref[...])
    x1_ref[...] = x1
    h2_ref[...] = (_rms(x1) * gpre_ref[...] * (1.0 + scf_ref[...]) + shf_ref[...]).astype(BF16)


def _post(od, osb, gate, x, ada3, wa, wb, wo, g_post, g_pre_ffn, tm, blocks_per_seq):
    r, d = x.shape
    rows_per_mod = ada3.shape[1]

    def mod_spec(col):
        if rows_per_mod == 1:
            return pl.BlockSpec((None, 1, d), lambda i: (i // blocks_per_seq, 0, col))
        return pl.BlockSpec((None, tm, d), lambda i: (0, i, col))

    row_spec = lambda w: pl.BlockSpec((tm, w), lambda i: (i, 0))
    const = lambda a: pl.BlockSpec(a.shape, lambda i: (0, 0), pipeline_mode=pl.Buffered(1))
    return pl.pallas_call(
        _post_kernel,
        out_shape=(jax.ShapeDtypeStruct((r, d), F32), jax.ShapeDtypeStruct((r, d), BF16)),
        grid=(r // tm,),
        in_specs=[row_spec(DIFF_V_W), row_spec(SB_W), row_spec(2 * D_MODEL), row_spec(d),
                  mod_spec(2), mod_spec(3), mod_spec(4),
                  const(wa), const(wb), const(wo), const(g_post), const(g_pre_ffn)],
        out_specs=(row_spec(d), row_spec(d)),
        compiler_params=_params(1),
        name="post_attn",
    )(od, osb, gate, x, ada3, ada3, ada3, wa, wb, wo, g_post, g_pre_ffn)


def _gelu_tanh(x):
    return 0.5 * x * (1.0 + jnp.tanh(math.sqrt(2.0 / math.pi) * (x + 0.044715 * (x * x * x))))


def _ffn_tail(u, um1, um2, cw_ref, cb_ref, wd_ref, x1_ref, gt_ref, gpost_ref, y_ref):
    conv = cb_ref[...] + um2 * cw_ref[0:1, :] + um1 * cw_ref[1:2, :] + u * cw_ref[2:3, :]
    y = (_gelu_tanh(conv[:, :D_FF]) * conv[:, D_FF:]).astype(BF16)
    f = _dot(y, wd_ref[...])
    y_ref[...] = x1_ref[...] + gt_ref[...] * (_rms(f) * gpost_ref[...])


def _ffn_prompt_kernel(h2_ref, x1_ref, gt_ref, wu_ref, cw_ref, cb_ref, wd_ref, gpost_ref,
                       y_ref, st_ref, ext_s, *, blocks_per_seq):
    tm = h2_ref.shape[0]
    i = pl.program_id(0)

    @pl.when(i % blocks_per_seq == 0)
    def _():
        ext_s[0:SUBLANES, :] = jnp.zeros((SUBLANES, ext_s.shape[1]), F32)

    u = _dot(h2_ref[...], wu_ref[...])
    ext_s[SUBLANES:, :] = u
    um1 = ext_s[SUBLANES - 1:SUBLANES - 1 + tm, :]
    um2 = ext_s[SUBLANES - 2:SUBLANES - 2 + tm, :]
    _ffn_tail(u, um1, um2, cw_ref, cb_ref, wd_ref, x1_ref, gt_ref, gpost_ref, y_ref)
    st_ref[...] = ext_s[tm + SUBLANES - (CONV_WIDTH - 1):, :]
    ext_s[0:SUBLANES, :] = ext_s[tm:tm + SUBLANES, :]


def _ffn_prompt(h2, x1, ada3, wu, cw, cb, wd, g_post, tm, blocks_per_seq):
    r, d = x1.shape
    n_seq = r // (tm * blocks_per_seq)
    const = lambda a: pl.BlockSpec(a.shape, lambda i: (0, 0), pipeline_mode=pl.Buffered(1))
    row_spec = lambda w: pl.BlockSpec((tm, w), lambda i: (i, 0))
    return pl.pallas_call(
        functools.partial(_ffn_prompt_kernel, blocks_per_seq=blocks_per_seq),
        out_shape=(jax.ShapeDtypeStruct((r, d), F32),
                   jax.ShapeDtypeStruct((n_seq, CONV_WIDTH - 1, 2 * D_FF), F32)),
        grid=(r // tm,),
        in_specs=[row_spec(d), row_spec(d),
                  pl.BlockSpec((None, 1, d), lambda i: (i // blocks_per_seq, 0, 5)),
                  const(wu), const(cw), const(cb), const(wd), const(g_post)],
        out_specs=(row_spec(d),
                   pl.BlockSpec((None, CONV_WIDTH - 1, 2 * D_FF), lambda i: (i // blocks_per_seq, 0, 0))),
        scratch_shapes=[pltpu.VMEM((tm + SUBLANES, 2 * D_FF), F32)],
        compiler_params=_params(1),
        name="ffn_prompt",
    )(h2, x1, ada3, wu, cw, cb, wd, g_post)


def _ffn_sample_kernel(h2_ref, x1_ref, gt_ref, st0_ref, st1_ref, wu_ref, cw_ref, cb_ref, wd_ref, gpost_ref,
                       y_ref, u_ref, *, t_new):
    u = _dot(h2_ref[...], wu_ref[...])
    u_ref[...] = u
    t_idx = lax.broadcasted_iota(jnp.int32, u.shape, 0) & (t_new - 1)
    st0 = st0_ref[...]
    st1 = st1_ref[...]
    um1 = jnp.where(t_idx >= 1, pltpu.roll(u, 1, 0), st1)
    um2 = jnp.where(t_idx >= 2, pltpu.roll(u, 2, 0), jnp.where(t_idx == 0, st0, st1))
    _ffn_tail(u, um1, um2, cw_ref, cb_ref, wd_ref, x1_ref, gt_ref, gpost_ref, y_ref)


def _ffn_sample(h2, x1, gt_rows, st0_rows, st1_rows, wu, cw, cb, wd, g_post, t_new):
    r, d = x1.shape
    return pl.pallas_call(
        functools.partial(_ffn_sample_kernel, t_new=t_new),
        out_shape=(jax.ShapeDtypeStruct((r, d), F32), jax.ShapeDtypeStruct((r, 2 * D_FF), F32)),
        compiler_params=pltpu.CompilerParams(vmem_limit_bytes=VMEM_LIMIT_BYTES),
        name="ffn_sample",
    )(h2, x1, gt_rows, st0_rows, st1_rows, wu, cw, cb, wd, g_post)


def _pick_tile(n, target):
    t = min(n, target)
    while n % t:
        t //= 2
    return t


def _stack_layers(vals):
    return vals[0][None] if len(vals) == 1 else jnp.stack(vals, axis=0)


def kernel(x_prompt, x_sample, c_prompt, c_sample, cache_diff_k, cache_diff_v, cache_sb_k, cache_sb_v, state_conv, page_table, w_ada, b_ada, g_pre_mix, g_post_mix, g_pre_ffn, g_post_ffn, w_in, lam_q1, lam_k1, lam_q2, lam_k2, g_subln, w_br_a, w_br_b, w_out, w_up, conv_w, conv_b, w_down):
    b, s, d = x_prompt.shape
    db, t_new, _ = x_sample.shape
    depth = w_in.shape[0]
    n_phys, page = cache_diff_k.shape[1], cache_diff_k.shape[2]
    n_past = page_table.shape[1] * page
    assert d == D_MODEL and t_new == SUBLANES and page == LANES

    tm = _pick_tile(s, 512)
    tq = _pick_tile(s, 256)
    bps = s // tm
    r_s = db * t_new

    tab_p = _rope_tables(jnp.arange(s, dtype=jnp.int32))
    tab_s = tuple(jnp.tile(t, (db, 1)) for t in _rope_tables(n_past + jnp.arange(t_new, dtype=jnp.int32)))

    ck = cache_diff_k.reshape(-1, 2 * DIFF_HEAD_DIM)
    cv = cache_diff_v.reshape(-1, DIFF_V_DIM)
    sb_view = lambda c: jnp.transpose(c, (0, 1, 3, 4, 2)).reshape(depth * n_phys, N_SB_HEADS, SB_HEAD_DIM, page)
    sk = sb_view(cache_sb_k)
    sv = sb_view(cache_sb_v)

    hp = x_prompt.reshape(b * s, d)
    hs = x_sample.reshape(r_s, d)
    c_all = jnp.concatenate([c_prompt, c_sample], axis=0)
    new_p = [[] for _ in range(5)]
    new_s = [[] for _ in range(5)]
    for l in range(depth):
        lam_init = 0.8 - 0.6 * math.exp(-0.3 * l)
        bf = lambda w: w[l].astype(BF16)
        row = lambda v: v[l].reshape(1, -1)
        lam_p = jnp.stack([lam_q1[l], lam_k1[l], lam_q2[l], lam_k2[l]], axis=0)
        ada = _ada(c_all, bf(w_ada), row(b_ada))
        ada_p = ada[:b].reshape(b, 1, 6 * d)
        ada_s = jnp.repeat(ada[b:], t_new, axis=0).reshape(1, r_s, 6 * d)
        w_in_l, wa, wb, wo, wu, wd = bf(w_in), bf(w_br_a), bf(w_br_b), bf(w_out), bf(w_up), bf(w_down)
        cw, cb = conv_w[l], row(conv_b)
        gsub = row(g_subln)

        qd, kdb, vdb, qs, ksb, vsb, gate, kd4, vd4, ks4, vs4 = _in_proj(
            hp, ada_p, row(g_pre_mix), w_in_l, tab_p, tm, bps, True)
        sb_cache = lambda a: jnp.transpose(a.reshape(b, N_SB_HEADS, SB_HEAD_DIM, s), (0, 3, 1, 2))
        r3 = lambda a: a.reshape(b, s, -1)
        od = _diff_prompt(r3(qd), r3(kdb), r3(vdb), lam_p, gsub, lam_init, tq)
        osb = _sb_prompt(r3(qs), r3(ksb), r3(vsb), tq)
        x1, h2 = _post(od.reshape(b * s, -1), osb.reshape(b * s, -1), gate, hp, ada_p, wa, wb, wo,
                       row(g_post_mix), row(g_pre_ffn), tm, bps)
        hp, conv_p = _ffn_prompt(h2, x1, ada_p, wu, cw, cb, wd, row(g_post_ffn), tm, bps)
        for lst, val in zip(new_p, (kd4.reshape(b, s, N_DIFF_HEADS, 2 * DIFF_HEAD_DIM),
                                    vd4.reshape(b, s, N_DIFF_HEADS, DIFF_V_DIM),
                                    sb_cache(ks4), sb_cache(vs4), conv_p)):
            lst.append(val)

        qd, kdb, vdb, qs, ksb, vsb, gate, kd4, vd4, ks4, vs4 = _in_proj(
            hs, ada_s, row(g_pre_mix), w_in_l, tab_s, r_s, 1, False)
        r3 = lambda a: a.reshape(db, t_new, -1)
        od, osb = _sample_attn(page_table, r3(qd), r3(kdb), r3(vdb), r3(qs), r3(ksb), r3(vsb),
                               ck, cv, sk, sv, lam_p, gsub, lam_init, l * n_phys)
        x1, h2 = _post(od.reshape(r_s, -1), osb.reshape(r_s, -1), gate, hs, ada_s, wa, wb, wo,
                       row(g_post_mix), row(g_pre_ffn), r_s, 1)
        st = state_conv[l]
        hs, u_s = _ffn_sample(h2, x1, ada_s[0, :, 5 * d:], jnp.repeat(st[:, 0], t_new, axis=0),
                              jnp.repeat(st[:, 1], t_new, axis=0), wu, cw, cb, wd, row(g_post_ffn), t_new)
        for lst, val in zip(new_s, (kd4.reshape(db, t_new, N_DIFF_HEADS, 2 * DIFF_HEAD_DIM),
                                    vd4.reshape(db, t_new, N_DIFF_HEADS, DIFF_V_DIM),
                                    ks4.reshape(db, t_new, N_SB_HEADS, SB_HEAD_DIM),
                                    vs4.reshape(db, t_new, N_SB_HEADS, SB_HEAD_DIM),
                                    u_s.reshape(db, t_new, 2 * D_FF)[:, t_new - (CONV_WIDTH - 1):])):
            lst.append(val)

    outs_p = [_stack_layers(a) for a in new_p]
    outs_s = [_stack_layers(a) for a in new_s]
    return (hp.reshape(b, s, d), hs.reshape(db, t_new, d), *outs_p, *outs_s)
```
